```python
import math
import jax, jax.numpy as jnp
from jax import lax
import numpy as np

D_MODEL = 1024
BATCH = 4
SEQ = 4096
DEPTH = 4

N_MIXERS = 2
N_A = (DEPTH + 1) // 2
N_B = DEPTH // 2
EPS = 1e-6
SC_WIDTH = 3
GDN_NK = 8
GDN_NV = 16
GDN_DK = 128
GDN_DV = 128
GDN_KEY_DIM = GDN_NK * GDN_DK
GDN_VALUE_DIM = GDN_NV * GDN_DV
GDN_QKV_DIM = 2 * GDN_KEY_DIM + GDN_VALUE_DIM
GDN_PROJ = GDN_QKV_DIM + GDN_VALUE_DIM + 2 * GDN_NV
GDN_CONV = 4
CHUNK = 64
D_FF = 2816
FFN_CONV = 3

kernel_name = 'hybrid_shortconv_gdn_convffn'


def rmsnorm(x, w):
    xf = x.astype(jnp.float32)
    y = xf * lax.rsqrt(jnp.mean(xf * xf, axis=-1, keepdims=True) + EPS)
    return (y * w.astype(jnp.float32)).astype(x.dtype)


def l2norm(x):
    return x * lax.rsqrt(jnp.sum(x * x, axis=-1, keepdims=True) + EPS)


def causal_dwconv(x, w):
    k_w, ch = w.shape
    return lax.conv_general_dilated(
        x, w.astype(x.dtype).reshape(k_w, 1, ch),
        window_strides=(1,), padding=[(k_w - 1, 0)],
        dimension_numbers=('NWC', 'WIO', 'NWC'),
        feature_group_count=ch)


def shortconv_mixer(h, w_in, conv_w, w_out):
    bcx = h @ w_in
    b_gate, c_gate, xv = jnp.split(bcx, 3, axis=-1)
    y = b_gate * causal_dwconv(c_gate * xv, conv_w)
    return y @ w_out


def gated_delta_rule_chunked(q, k, v, g, beta):
    bsz, s, h, dk = q.shape
    dv = v.shape[-1]
    n = s // CHUNK

    def chunks(t):
        return t.reshape(bsz, n, CHUNK, h, t.shape[-1]).transpose(1, 0, 3, 2, 4)

    qc, kc, vc = chunks(q), chunks(k), chunks(v)
    gc = jnp.cumsum(g.reshape(bsz, n, CHUNK, h).transpose(1, 0, 3, 2), axis=-1)
    bc = beta.reshape(bsz, n, CHUNK, h).transpose(1, 0, 3, 2)

    tril = jnp.tril(jnp.ones((CHUNK, CHUNK), dtype=bool))
    strict = jnp.tril(jnp.ones((CHUNK, CHUNK), dtype=bool), -1)
    diff = gc[..., :, None] - gc[..., None, :]
    decay = jnp.where(tril, jnp.exp(jnp.where(tril, diff, 0.0)), 0.0)

    kk = jnp.einsum('nbhid,nbhjd->nbhij', kc, kc)
    low = jnp.where(strict, bc[..., :, None] * kk * decay, 0.0)
    a_mat = jnp.eye(CHUNK, dtype=low.dtype) + low
    rhs = jnp.concatenate([vc * bc[..., None],
                           kc * (bc * jnp.exp(gc))[..., None]], axis=-1)
    sol = lax.linalg.triangular_solve(a_mat, rhs, left_side=True, lower=True,
                                      unit_diagonal=True)
    u, w = sol[..., :dv], sol[..., dv:]
    qk_intra = jnp.einsum('nbhid,nbhjd->nbhij', qc, kc) * decay
    q_dec = qc * jnp.exp(gc)[..., None]
    g_last = gc[..., -1]
    k_dec = kc * jnp.exp(g_last[..., None] - gc)[..., None]

    def step(state, xs):
        q_i, k_i, u_i, w_i, qk_i, gl_i = xs
        v_new = u_i - jnp.einsum('bhck,bhkv->bhcv', w_i, state)
        o_i = (jnp.einsum('bhck,bhkv->bhcv', q_i, state)
               + jnp.einsum('bhij,bhjv->bhiv', qk_i, v_new))
        state = (state * jnp.exp(gl_i)[..., None, None]
                 + jnp.einsum('bhck,bhcv->bhkv', k_i, v_new))
        return state, o_i

    state0 = jnp.zeros((bsz, h, dk, dv), dtype=q.dtype)
    _, o = lax.scan(step, state0, (q_dec, k_dec, u, w, qk_intra, g_last))
    return o.transpose(1, 0, 3, 2, 4).reshape(bsz, s, h, dv)


def gdn_mixer(h, w_in, conv_w, a_log, dt_bias, norm_w, w_out):
    bsz, s, _ = h.shape
    proj = h @ w_in
    qkv = proj[..., :GDN_QKV_DIM]
    z = proj[..., GDN_QKV_DIM:GDN_QKV_DIM + GDN_VALUE_DIM]
    a = proj[..., GDN_QKV_DIM + GDN_VALUE_DIM:GDN_QKV_DIM + GDN_VALUE_DIM + GDN_NV]
    b = proj[..., GDN_QKV_DIM + GDN_VALUE_DIM + GDN_NV:]
    qkv = jax.nn.silu(causal_dwconv(qkv, conv_w))
    q = qkv[..., :GDN_KEY_DIM].reshape(bsz, s, GDN_NK, GDN_DK).astype(jnp.float32)
    k = qkv[..., GDN_KEY_DIM:2 * GDN_KEY_DIM].reshape(bsz, s, GDN_NK, GDN_DK).astype(jnp.float32)
    v = qkv[..., 2 * GDN_KEY_DIM:].reshape(bsz, s, GDN_NV, GDN_DV).astype(jnp.float32)
    q = l2norm(q) * (GDN_DK ** -0.5)
    k = l2norm(k)
    rep = GDN_NV // GDN_NK
    q = jnp.repeat(q, rep, axis=2)
    k = jnp.repeat(k, rep, axis=2)
    beta = jax.nn.sigmoid(b.astype(jnp.float32))
    g = -jnp.exp(a_log.astype(jnp.float32)) * jax.nn.softplus(
        a.astype(jnp.float32) + dt_bias.astype(jnp.float32))
    o = gated_delta_rule_chunked(q, k, v, g, beta)
    o = rmsnorm(o, norm_w) * jax.nn.silu(z.reshape(bsz, s, GDN_NV, GDN_DV).astype(jnp.float32))
    return o.reshape(bsz, s, GDN_VALUE_DIM).astype(h.dtype) @ w_out


def conv_ffn(h, w_up, conv_w, w_down):
    u = causal_dwconv(h @ w_up, conv_w)
    gate, val = jnp.split(u, 2, axis=-1)
    return (jax.nn.silu(gate) * val) @ w_down


def setup_inputs(seed: int = 0) -> dict:
    key = jax.random.key(seed)
    ks = jax.random.split(key, 20)
    f32 = jnp.float32

    def nrm(k, shape, fan_in):
        return jax.random.normal(k, shape, f32) * (fan_in ** -0.5)

    def gain(k, shape):
        return 1.0 + 0.02 * jax.random.normal(k, shape, f32)

    x = jax.random.normal(ks[0], (BATCH, SEQ, D_MODEL), f32)
    norm_mix = gain(ks[1], (DEPTH, D_MODEL))
    norm_ffn = gain(ks[2], (DEPTH, D_MODEL))
    sc_w_in = nrm(ks[3], (N_A, D_MODEL, 3 * D_MODEL), D_MODEL)
    sc_conv = nrm(ks[4], (N_A, SC_WIDTH, D_MODEL), SC_WIDTH)
    sc_w_out = nrm(ks[5], (N_A, D_MODEL, D_MODEL), D_MODEL)
    gdn_w_in = nrm(ks[6], (N_B, D_MODEL, GDN_PROJ), D_MODEL)
    gdn_conv = nrm(ks[7], (N_B, GDN_CONV, GDN_QKV_DIM), GDN_CONV)
    gdn_a_log = jnp.log(jax.random.uniform(ks[8], (N_B, GDN_NV), f32, 1.0, 16.0))
    dt = jnp.exp(jax.random.uniform(ks[9], (N_B, GDN_NV), f32,
                                    math.log(0.001), math.log(0.1)))
    gdn_dt_bias = dt + jnp.log(-jnp.expm1(-dt))
    gdn_norm = gain(ks[10], (N_B, GDN_DV))
    gdn_w_out = nrm(ks[11], (N_B, GDN_VALUE_DIM, D_MODEL), GDN_VALUE_DIM)
    ffn_w_up = nrm(ks[12], (DEPTH, D_MODEL, 2 * D_FF), D_MODEL)
    ffn_conv = nrm(ks[13], (DEPTH, FFN_CONV, 2 * D_FF), FFN_CONV)
    ffn_w_down = nrm(ks[14], (DEPTH, D_FF, D_MODEL), D_FF)
    norm_final = gain(ks[15], (D_MODEL,))
    return {'x': x, 'norm_mix': norm_mix, 'norm_ffn': norm_ffn,
            'sc_w_in': sc_w_in, 'sc_conv': sc_conv, 'sc_w_out': sc_w_out,
            'gdn_w_in': gdn_w_in, 'gdn_conv': gdn_conv, 'gdn_a_log': gdn_a_log,
            'gdn_dt_bias': gdn_dt_bias, 'gdn_norm': gdn_norm, 'gdn_w_out': gdn_w_out,
            'ffn_w_up': ffn_w_up, 'ffn_conv': ffn_conv, 'ffn_w_down': ffn_w_down,
            'norm_final': norm_final}


def reference(x, norm_mix, norm_ffn, sc_w_in, sc_conv, sc_w_out,
              gdn_w_in, gdn_conv, gdn_a_log, gdn_dt_bias, gdn_norm, gdn_w_out,
              ffn_w_up, ffn_conv, ffn_w_down, norm_final):
    for i in range(DEPTH):
        h = rmsnorm(x, norm_mix[i])
        j = i // N_MIXERS
        if i % N_MIXERS == 0:
            mix = shortconv_mixer(h, sc_w_in[j], sc_conv[j], sc_w_out[j])
        else:
            mix = gdn_mixer(h, gdn_w_in[j], gdn_conv[j], gdn_a_log[j], gdn_dt_bias[j],
                            gdn_norm[j], gdn_w_out[j])
        x = x + mix
        h = rmsnorm(x, norm_ffn[i])
        x = x + conv_ffn(h, ffn_w_up[i], ffn_conv[i], ffn_w_down[i])
    return rmsnorm(x, norm_final)
```

```python
import functools

import jax
import jax.numpy as jnp
from jax import lax
from jax.experimental import pallas as pl
from jax.experimental.pallas import tpu as pltpu

F32 = jnp.float32
BF16 = jnp.bfloat16

D_MODEL = 1024
DEPTH = 4
EPS = 1e-6
GDN_NK = 8
GDN_NV = 16
GDN_DK = 128
GDN_DV = 128
GDN_KEY_DIM = GDN_NK * GDN_DK
GDN_VALUE_DIM = GDN_NV * GDN_DV
GDN_QKV_DIM = 2 * GDN_KEY_DIM + GDN_VALUE_DIM
D_FF = 2816

LANES = 128
HALO = 16
TM = 512
CHUNK = 128
FF_CHUNK = D_FF // 2
VMEM_LIMIT = 56 * 1024 * 1024


def _rms(x, w):
    ms = jnp.mean(x * x, axis=-1, keepdims=True)
    return x * lax.rsqrt(ms + EPS) * w


def _silu(x):
    return x * jax.nn.sigmoid(x)


def _dot(a, b):
    return jnp.dot(a, b, preferred_element_type=F32)


def _dot_nt(a, b):
    return lax.dot_general(a, b, (((1,), (1,)), ((), ())), preferred_element_type=F32)


def _dot_tn(a, b):
    return lax.dot_general(a, b, (((0,), (0,)), ((), ())), preferred_element_type=F32)


def _causal_conv(p, carry, w):
    taps = w.shape[0]
    top = p[:HALO]
    acc = p * w[taps - 1:taps]
    acc_top = top * w[taps - 1:taps]
    row = lax.broadcasted_iota(jnp.int32, top.shape, 0)
    for s in range(1, taps):
        wk = w[taps - 1 - s:taps - s]
        rolled = pltpu.roll(p, s, 0)
        acc = acc + rolled * wk
        top_s = jnp.where(row < s, pltpu.roll(carry, s, 0), rolled[:HALO])
        acc_top = acc_top + top_s * wk
    return jnp.concatenate([acc_top, acc[HALO:]], axis=0)


def _zero_carry_at_sequence_start(carry_ref):
    @pl.when(pl.program_id(1) == 0)
    def _():
        carry_ref[...] = jnp.zeros_like(carry_ref)


def _shortconv_kernel(x_ref, nw_ref, win_ref, cw_ref, wout_ref, o_ref, carry_ref):
    _zero_carry_at_sequence_start(carry_ref)
    x = x_ref[0]
    h = _rms(x, nw_ref[...]).astype(BF16)
    d = D_MODEL
    b_gate = _dot(h, win_ref[:, 0:d])
    c_gate = _dot(h, win_ref[:, d:2 * d])
    xv = _dot(h, win_ref[:, 2 * d:3 * d])
    p = c_gate * xv
    conv = _causal_conv(p, carry_ref[...], cw_ref[...])
    carry_ref[...] = p[TM - HALO:]
    y = (b_gate * conv).astype(BF16)
    o_ref[0] = x + _dot(y, wout_ref[...])


def _ffn_kernel(x_ref, nw_ref, wup_ref, cw_ref, wdn_ref, fw_ref, o_ref, carry_ref, *, final):
    _zero_carry_at_sequence_start(carry_ref)
    x = x_ref[0]
    h = _rms(x, nw_ref[...]).astype(BF16)
    acc = x
    for c in range(D_FF // FF_CHUNK):
        g_sl = slice(c * FF_CHUNK, (c + 1) * FF_CHUNK)
        v_sl = slice(D_FF + c * FF_CHUNK, D_FF + (c + 1) * FF_CHUNK)
        ug = _dot(h, wup_ref[:, g_sl])
        uv = _dot(h, wup_ref[:, v_sl])
        cg = _causal_conv(ug, carry_ref[:, g_sl], cw_ref[:, g_sl])
        cv = _causal_conv(uv, carry_ref[:, v_sl], cw_ref[:, v_sl])
        carry_ref[:, g_sl] = ug[TM - HALO:]
        carry_ref[:, v_sl] = uv[TM - HALO:]
        act = (_silu(cg) * cv).astype(BF16)
        acc = acc + _dot(act, wdn_ref[g_sl, :])
    if final:
        acc = _rms(acc, fw_ref[...])
    o_ref[0] = acc


def _split3_bf16(x):
    hi = x.astype(BF16)
    r1 = x - hi.astype(F32)
    mid = r1.astype(BF16)
    lo = (r1 - mid.astype(F32)).astype(BF16)
    return hi, mid, lo


def _gdn_proj_kernel(x_ref, nw_ref, wqkv_ref, wz_ref, wab_ref, cw_ref, alog_ref, dtb_ref,
                     q_ref, k_ref, v_ref, z_ref, gcol_ref, grow_ref, carry_ref):
    _zero_carry_at_sequence_start(carry_ref)
    x = x_ref[0]
    h = _rms(x, nw_ref[...]).astype(BF16)
    group = GDN_KEY_DIM
    heads_per_group = group // LANES
    for c in range(GDN_QKV_DIM // group):
        sl = slice(c * group, (c + 1) * group)
        pre = _dot(h, wqkv_ref[:, sl])
        act = _silu(_causal_conv(pre, carry_ref[:, sl], cw_ref[:, sl]))
        carry_ref[:, sl] = pre[TM - HALO:]
        for hh in range(heads_per_group):
            ah = act[:, hh * LANES:(hh + 1) * LANES]
            if c < 2:
                ah = ah * lax.rsqrt(jnp.sum(ah * ah, axis=-1, keepdims=True) + EPS)
            if c == 0:
                q_ref[0, hh] = ah * (GDN_DK ** -0.5)
            elif c == 1:
                k_ref[0, hh] = ah
            else:
                v_ref[0, (c - 2) * heads_per_group + hh] = ah
    z_ref[0] = _dot(h, wz_ref[...])

    ab = _dot(h, wab_ref[...])
    lane = lax.broadcasted_iota(jnp.int32, ab.shape, 1)
    sp_in = ab + dtb_ref[...]
    softplus = jnp.maximum(sp_in, 0.0) + jnp.log1p(jnp.exp(-jnp.abs(sp_in)))
    g = -jnp.exp(alog_ref[...]) * softplus
    gate = jnp.where(lane < GDN_NV, g, jax.nn.sigmoid(ab))

    r = lax.broadcasted_iota(jnp.int32, (CHUNK, CHUNK), 0)
    cc = lax.broadcasted_iota(jnp.int32, (CHUNK, CHUNK), 1)
    tri = (r >= cc).astype(BF16)
    for blk in range(TM // CHUNK):
        rows = slice(blk * CHUNK, (blk + 1) * CHUNK)
        gate_blk = gate[rows]
        hi, mid, lo = _split3_bf16(gate_blk)
        gc = _dot(tri, hi) + _dot(tri, mid) + _dot(tri, lo)
        gcol_ref[0, rows, :] = jnp.where(cc < GDN_NV, gc, gate_blk)
        grow_ref[0, :, rows] = gc.T[:GDN_NV]


def _gdn_rec_kernel(q_ref, k_ref, v_ref, gcol_ref, grow_ref, o_ref, state_ref):
    @pl.when(pl.program_id(1) == 0)
    def _():
        state_ref[...] = jnp.zeros_like(state_ref)

    n = CHUNK
    r = lax.broadcasted_iota(jnp.int32, (n, n), 0)
    c = lax.broadcasted_iota(jnp.int32, (n, n), 1)
    tril = r >= c
    strict = r > c
    eye = (r == c).astype(F32)
    levels = []
    m_log = 0
    while (1 << m_log) < n:
        same_pair = (r >> (m_log + 1)) == (c >> (m_log + 1))
        levels.append(same_pair & (((r >> m_log) & 1) == 1) & (((c >> m_log) & 1) == 0))
        m_log += 1

    gcol = gcol_ref[0]
    grow = grow_ref[0]

    for hk in range(GDN_NK):
        q = q_ref[0, hk]
        k = k_ref[0, hk]
        qb = q.astype(BF16)
        kb = k.astype(BF16)
        kk = _dot_nt(kb, kb)
        qk = _dot_nt(qb, kb)
        for j in range(GDN_NV // GDN_NK):
            hv = hk * (GDN_NV // GDN_NK) + j
            gc = gcol[:, hv:hv + 1]
            beta = gcol[:, GDN_NV + hv:GDN_NV + hv + 1]
            gr = grow[hv:hv + 1, :]
            g_last = gcol[n - 1:n, hv:hv + 1]
            diff = gc - gr
            decay = jnp.where(tril, jnp.exp(jnp.where(tril, diff, 0.0)), 0.0)
            low = jnp.where(strict, beta * kk * decay, 0.0)

            x_inv = eye - jnp.where(levels[0], low, 0.0)
            for mask in levels[1:]:
                cm = jnp.where(mask, low, 0.0).astype(BF16)
                xb = x_inv.astype(BF16)
                x_inv = x_inv - _dot(xb, _dot(cm, xb).astype(BF16))

            v = v_ref[0, hv]
            e_gc = jnp.exp(gc)
            rhs = jnp.concatenate([v * beta, k * (beta * e_gc)], axis=1).astype(BF16)
            sol = _dot(x_inv.astype(BF16), rhs)
            u = sol[:, :GDN_DV]
            w = sol[:, GDN_DV:]

            state = state_ref[hv]
            state_b = state.astype(BF16)
            v_new = u - _dot(w.astype(BF16), state_b)
            v_new_b = v_new.astype(BF16)
            o = _dot((q * e_gc).astype(BF16), state_b) + _dot((qk * decay).astype(BF16), v_new_b)
            k_dec = (k * jnp.exp(g_last - gc)).astype(BF16)
            state_ref[hv] = state * jnp.exp(g_last) + _dot_tn(k_dec, v_new_b)
            o_ref[0, :, hv * GDN_DV:(hv + 1) * GDN_DV] = o


def _gdn_out_kernel(o_ref, z_ref, x_ref, nw_ref, wout_ref, out_ref, y_ref):
    for hv in range(GDN_NV):
        sl = slice(hv * GDN_DV, (hv + 1) * GDN_DV)
        o = o_ref[0, :, sl]
        y = _rms(o, nw_ref[...]) * _silu(z_ref[0, :, sl])
        y_ref[:, sl] = y.astype(BF16)
    out_ref[0] = x_ref[0] + _dot(y_ref[...], wout_ref[...])


def _params():
    return pltpu.CompilerParams(dimension_semantics=("arbitrary", "arbitrary"),
                                vmem_limit_bytes=VMEM_LIMIT)


def _tok_spec(width):
    return pl.BlockSpec((1, TM, width), lambda b, t: (b, t, 0))


def _const_spec(shape):
    nd = len(shape)
    return pl.BlockSpec(shape, lambda b, t: (0,) * nd, pipeline_mode=pl.Buffered(1))


def _shortconv_layer(x, nw, w_in, conv_w, w_out):
    bsz, s, d = x.shape
    return pl.pallas_call(
        _shortconv_kernel,
        grid=(bsz, s // TM),
        in_specs=[_tok_spec(d), _const_spec((1, d)), _const_spec(w_in.shape),
                  _const_spec(conv_w.shape), _const_spec(w_out.shape)],
        out_specs=_tok_spec(d),
        out_shape=jax.ShapeDtypeStruct(x.shape, F32),
        scratch_shapes=[pltpu.VMEM((HALO, d), F32)],
        compiler_params=_params(),
        name="shortconv_mixer",
    )(x, nw.reshape(1, d), w_in.astype(BF16), conv_w, w_out.astype(BF16))


def _ffn_layer(x, nw, w_up, conv_w, w_down, final_w, final):
    bsz, s, d = x.shape
    return pl.pallas_call(
        functools.partial(_ffn_kernel, final=final),
        grid=(bsz, s // TM),
        in_specs=[_tok_spec(d), _const_spec((1, d)), _const_spec(w_up.shape),
                  _const_spec(conv_w.shape), _const_spec(w_down.shape), _const_spec((1, d))],
        out_specs=_tok_spec(d),
        out_shape=jax.ShapeDtypeStruct(x.shape, F32),
        scratch_shapes=[pltpu.VMEM((HALO, 2 * D_FF), F32)],
        compiler_params=_params(),
        name="conv_ffn",
    )(x, nw.reshape(1, d), w_up.astype(BF16), conv_w, w_down.astype(BF16), final_w.reshape(1, d))


def _pad_lanes(v, offset=0):
    return jnp.zeros((1, LANES), F32).at[0, offset:offset + v.shape[0]].set(v)


def _gdn_layer(x, nw, w_in, conv_w, a_log, dt_bias, norm_w, w_out):
    bsz, s, d = x.shape
    w_qkv = w_in[:, :GDN_QKV_DIM].astype(BF16)
    w_z = w_in[:, GDN_QKV_DIM:GDN_QKV_DIM + GDN_VALUE_DIM].astype(BF16)
    w_ab = jnp.zeros((d, LANES), F32).at[:, :2 * GDN_NV].set(
        w_in[:, GDN_QKV_DIM + GDN_VALUE_DIM:]).astype(BF16)

    head_spec = lambda nh: pl.BlockSpec((1, nh, TM, LANES), lambda b, t: (b, 0, t, 0))
    q, k, v, z, gcol, grow = pl.pallas_call(
        _gdn_proj_kernel,
        grid=(bsz, s // TM),
        in_specs=[_tok_spec(d), _const_spec((1, d)), _const_spec(w_qkv.shape), _const_spec(w_z.shape),
                  _const_spec(w_ab.shape), _const_spec(conv_w.shape),
                  _const_spec((1, LANES)), _const_spec((1, LANES))],
        out_specs=[head_spec(GDN_NK), head_spec(GDN_NK), head_spec(GDN_NV), _tok_spec(GDN_VALUE_DIM),
                   _tok_spec(LANES), pl.BlockSpec((1, GDN_NV, TM), lambda b, t: (b, 0, t))],
        out_shape=[jax.ShapeDtypeStruct((bsz, GDN_NK, s, GDN_DK), F32),
                   jax.ShapeDtypeStruct((bsz, GDN_NK, s, GDN_DK), F32),
                   jax.ShapeDtypeStruct((bsz, GDN_NV, s, GDN_DV), F32),
                   jax.ShapeDtypeStruct((bsz, s, GDN_VALUE_DIM), F32),
                   jax.ShapeDtypeStruct((bsz, s, LANES), F32),
                   jax.ShapeDtypeStruct((bsz, GDN_NV, s), F32)],
        scratch_shapes=[pltpu.VMEM((HALO, GDN_QKV_DIM), F32)],
        compiler_params=_params(),
        name="gdn_proj",
    )(x, nw.reshape(1, d), w_qkv, w_z, w_ab, conv_w, _pad_lanes(a_log), _pad_lanes(dt_bias))

    chunk_heads = lambda nh: pl.BlockSpec((1, nh, CHUNK, LANES), lambda b, t: (b, 0, t, 0))
    o = pl.pallas_call(
        _gdn_rec_kernel,
        grid=(bsz, s // CHUNK),
        in_specs=[chunk_heads(GDN_NK), chunk_heads(GDN_NK), chunk_heads(GDN_NV),
                  pl.BlockSpec((1, CHUNK, LANES), lambda b, t: (b, t, 0)),
                  pl.BlockSpec((1, GDN_NV, CHUNK), lambda b, t: (b, 0, t))],
        out_specs=pl.BlockSpec((1, CHUNK, GDN_VALUE_DIM), lambda b, t: (b, t, 0)),
        out_shape=jax.ShapeDtypeStruct((bsz, s, GDN_VALUE_DIM), F32),
        scratch_shapes=[pltpu.VMEM((GDN_NV, GDN_DK, GDN_DV), F32)],
        compiler_params=_params(),
        name="gdn_recurrence",
    )(q, k, v, gcol, grow)

    return pl.pallas_call(
        _gdn_out_kernel,
        grid=(bsz, s // TM),
        in_specs=[_tok_spec(GDN_VALUE_DIM), _tok_spec(GDN_VALUE_DIM), _tok_spec(d),
                  _const_spec((1, GDN_DV)), _const_spec(w_out.shape)],
        out_specs=_tok_spec(d),
        out_shape=jax.ShapeDtypeStruct(x.shape, F32),
        scratch_shapes=[pltpu.VMEM((TM, GDN_VALUE_DIM), BF16)],
        compiler_params=_params(),
        name="gdn_out",
    )(o, z, x, norm_w.reshape(1, GDN_DV), w_out.astype(BF16))


def kernel(x, norm_mix, norm_ffn, sc_w_in, sc_conv, sc_w_out, gdn_w_in, gdn_conv, gdn_a_log,
           gdn_dt_bias, gdn_norm, gdn_w_out, ffn_w_up, ffn_conv, ffn_w_down, norm_final):
    for i in range(DEPTH):
        j = i // 2
        if i % 2 == 0:
            x = _shortconv_layer(x, norm_mix[i], sc_w_in[j], sc_conv[j], sc_w_out[j])
        else:
            x = _gdn_layer(x, norm_mix[i], gdn_w_in[j], gdn_conv[j], gdn_a_log[j], gdn_dt_bias[j],
                           gdn_norm[j], gdn_w_out[j])
        x = _ffn_layer(x, norm_ffn[i], ffn_w_up[i], ffn_conv[i], ffn_w_down[i], norm_final,
                       final=(i == DEPTH - 1))
    return x
```

```python
import functools

import jax
import jax.numpy as jnp
from jax import lax
from jax.experimental import pallas as pl
from jax.experimental.pallas import tpu as pltpu

F32 = jnp.float32
BF16 = jnp.bfloat16

D_MODEL = 1024
DEPTH = 4
EPS = 1e-6
GDN_NK = 8
GDN_NV = 16
GDN_DK = 128
GDN_DV = 128
GDN_KEY_DIM = GDN_NK * GDN_DK
GDN_VALUE_DIM = GDN_NV * GDN_DV
GDN_QKV_DIM = 2 * GDN_KEY_DIM + GDN_VALUE_DIM
D_FF = 2816

LANES = 128
HALO = 16
TM = 512
CHUNK = 128
FF_CHUNK = D_FF // 2
VMEM_LIMIT = 56 * 1024 * 1024


def _rms(x, w):
    ms = jnp.mean(x * x, axis=-1, keepdims=True)
    return x * lax.rsqrt(ms + EPS) * w


def _silu(x):
    return x * jax.nn.sigmoid(x)


def _dot(a, b):
    return jnp.dot(a, b, preferred_element_type=F32)


def _dot_nt(a, b):
    return lax.dot_general(a, b, (((1,), (1,)), ((), ())), preferred_element_type=F32)


def _dot_tn(a, b):
    return lax.dot_general(a, b, (((0,), (0,)), ((), ())), preferred_element_type=F32)


def _causal_conv(p, carry, w):
    taps = w.shape[0]
    top = p[:HALO]
    acc = p * w[taps - 1:taps]
    acc_top = top * w[taps - 1:taps]
    row = lax.broadcasted_iota(jnp.int32, top.shape, 0)
    for s in range(1, taps):
        wk = w[taps - 1 - s:taps - s]
        rolled = pltpu.roll(p, s, 0)
        acc = acc + rolled * wk
        top_s = jnp.where(row < s, pltpu.roll(carry, s, 0), rolled[:HALO])
        acc_top = acc_top + top_s * wk
    return jnp.concatenate([acc_top, acc[HALO:]], axis=0)


def _zero_carry_at_sequence_start(carry_ref):
    @pl.when(pl.program_id(1) == 0)
    def _():
        carry_ref[...] = jnp.zeros_like(carry_ref)


def _shortconv_kernel(x_ref, nw_ref, win_ref, cw_ref, wout_ref, o_ref, carry_ref):
    _zero_carry_at_sequence_start(carry_ref)
    x = x_ref[0]
    h = _rms(x, nw_ref[...]).astype(BF16)
    d = D_MODEL
    b_gate = _dot(h, win_ref[:, 0:d])
    c_gate = _dot(h, win_ref[:, d:2 * d])
    xv = _dot(h, win_ref[:, 2 * d:3 * d])
    p = c_gate * xv
    conv = _causal_conv(p, carry_ref[...], cw_ref[...])
    carry_ref[...] = p[TM - HALO:]
    y = (b_gate * conv).astype(BF16)
    o_ref[0] = x + _dot(y, wout_ref[...])


def _ffn_kernel(x_ref, nw_ref, wup_ref, cw_ref, wdn_ref, fw_ref, o_ref, carry_ref, *, final):
    _zero_carry_at_sequence_start(carry_ref)
    x = x_ref[0]
    h = _rms(x, nw_ref[...]).astype(BF16)
    acc = x
    for c in range(D_FF // FF_CHUNK):
        g_sl = slice(c * FF_CHUNK, (c + 1) * FF_CHUNK)
        v_sl = slice(D_FF + c * FF_CHUNK, D_FF + (c + 1) * FF_CHUNK)
        ug = _dot(h, wup_ref[:, g_sl])
        uv = _dot(h, wup_ref[:, v_sl])
        cg = _causal_conv(ug, carry_ref[:, g_sl], cw_ref[:, g_sl])
        cv = _causal_conv(uv, carry_ref[:, v_sl], cw_ref[:, v_sl])
        carry_ref[:, g_sl] = ug[TM - HALO:]
        carry_ref[:, v_sl] = uv[TM - HALO:]
        act = (_silu(cg) * cv).astype(BF16)
        acc = acc + _dot(act, wdn_ref[g_sl, :])
    if final:
        acc = _rms(acc, fw_ref[...])
    o_ref[0] = acc


def _split3_bf16(x):
    hi = x.astype(BF16)
    r1 = x - hi.astype(F32)
    mid = r1.astype(BF16)
    lo = (r1 - mid.astype(F32)).astype(BF16)
    return hi, mid, lo


def _gdn_proj_kernel(x_ref, nw_ref, wqkv_ref, wz_ref, wab_ref, cw_ref, alog_ref, dtb_ref,
                     q_ref, k_ref, v_ref, z_ref, gcol_ref, grow_ref, carry_ref):
    _zero_carry_at_sequence_start(carry_ref)
    x = x_ref[0]
    h = _rms(x, nw_ref[...]).astype(BF16)
    group = GDN_KEY_DIM
    heads_per_group = group // LANES
    for c in range(GDN_QKV_DIM // group):
        sl = slice(c * group, (c + 1) * group)
        pre = _dot(h, wqkv_ref[:, sl])
        act = _silu(_causal_conv(pre, carry_ref[:, sl], cw_ref[:, sl]))
        carry_ref[:, sl] = pre[TM - HALO:]
        for hh in range(heads_per_group):
            ah = act[:, hh * LANES:(hh + 1) * LANES]
            if c < 2:
                ah = ah * lax.rsqrt(jnp.sum(ah * ah, axis=-1, keepdims=True) + EPS)
            if c == 0:
                q_ref[0, hh] = ah * (GDN_DK ** -0.5)
            elif c == 1:
                k_ref[0, hh] = ah
            else:
                v_ref[0, (c - 2) * heads_per_group + hh] = ah
    z_ref[0] = _dot(h, wz_ref[...])

    ab = _dot(h, wab_ref[...])
    lane = lax.broadcasted_iota(jnp.int32, ab.shape, 1)
    sp_in = ab + dtb_ref[...]
    softplus = jnp.maximum(sp_in, 0.0) + jnp.log1p(jnp.exp(-jnp.abs(sp_in)))
    g = -jnp.exp(alog_ref[...]) * softplus
    gate = jnp.where(lane < GDN_NV, g, jax.nn.sigmoid(ab))

    r = lax.broadcasted_iota(jnp.int32, (CHUNK, CHUNK), 0)
    cc = lax.broadcasted_iota(jnp.int32, (CHUNK, CHUNK), 1)
    tri = (r >= cc).astype(BF16)
    for blk in range(TM // CHUNK):
        rows = slice(blk * CHUNK, (blk + 1) * CHUNK)
        gate_blk = gate[rows]
        hi, mid, lo = _split3_bf16(gate_blk)
        gc = _dot(tri, hi) + _dot(tri, mid) + _dot(tri, lo)
        gcol_ref[0, rows, :] = jnp.where(cc < GDN_NV, gc, gate_blk)
        grow_ref[0, :, rows] = gc.T[:GDN_NV]


def _gdn_rec_kernel(q_ref, k_ref, v_ref, gcol_ref, grow_ref, o_ref, state_ref):
    @pl.when(pl.program_id(1) == 0)
    def _():
        state_ref[...] = jnp.zeros_like(state_ref)

    n = CHUNK
    r = lax.broadcasted_iota(jnp.int32, (n, n), 0)
    c = lax.broadcasted_iota(jnp.int32, (n, n), 1)
    tril = r >= c
    strict = r > c
    eye = (r == c).astype(F32)
    levels = []
    m_log = 0
    while (1 << m_log) < n:
        same_pair = (r >> (m_log + 1)) == (c >> (m_log + 1))
        levels.append(same_pair & (((r >> m_log) & 1) == 1) & (((c >> m_log) & 1) == 0))
        m_log += 1

    gcol = gcol_ref[0]
    grow = grow_ref[0]

    rep = GDN_NV // GDN_NK
    heads = range(GDN_NV)
    q = [q_ref[0, hk] for hk in range(GDN_NK)]
    k = [k_ref[0, hk] for hk in range(GDN_NK)]
    kb = [kh.astype(BF16) for kh in k]
    kk = [_dot_nt(kb[hk], kb[hk]) for hk in range(GDN_NK)]
    qk = [_dot_nt(q[hk].astype(BF16), kb[hk]) for hk in range(GDN_NK)]

    gc = [gcol[:, hv:hv + 1] for hv in heads]
    beta = [gcol[:, GDN_NV + hv:GDN_NV + hv + 1] for hv in heads]
    g_last = [gcol[n - 1:n, hv:hv + 1] for hv in heads]
    decay, low = [], []
    for hv in heads:
        diff = gc[hv] - grow[hv:hv + 1, :]
        decay.append(jnp.where(tril, jnp.exp(jnp.where(tril, diff, 0.0)), 0.0))
        low.append(jnp.where(strict, beta[hv] * kk[hv // rep] * decay[hv], 0.0))

    x_inv = [eye - jnp.where(levels[0], low[hv], 0.0) for hv in heads]
    for mask in levels[1:]:
        xb = [x_inv[hv].astype(BF16) for hv in heads]
        cx = [_dot(jnp.where(mask, low[hv], 0.0).astype(BF16), xb[hv]).astype(BF16) for hv in heads]
        x_inv = [x_inv[hv] - _dot(xb[hv], cx[hv]) for hv in heads]

    e_gc = [jnp.exp(gc[hv]) for hv in heads]
    sol = []
    for hv in heads:
        rhs = jnp.concatenate([v_ref[0, hv] * beta[hv], k[hv // rep] * (beta[hv] * e_gc[hv])], axis=1)
        sol.append(_dot(x_inv[hv].astype(BF16), rhs.astype(BF16)))

    state = [state_ref[hv] for hv in heads]
    state_b = [s.astype(BF16) for s in state]
    v_new_b = [(sol[hv][:, :GDN_DV] - _dot(sol[hv][:, GDN_DV:].astype(BF16), state_b[hv])).astype(BF16)
               for hv in heads]
    for hv in heads:
        hk = hv // rep
        o = (_dot((q[hk] * e_gc[hv]).astype(BF16), state_b[hv])
             + _dot((qk[hk] * decay[hv]).astype(BF16), v_new_b[hv]))
        o_ref[0, :, hv * GDN_DV:(hv + 1) * GDN_DV] = o
    for hv in heads:
        k_dec = (k[hv // rep] * jnp.exp(g_last[hv] - gc[hv])).astype(BF16)
        state_ref[hv] = state[hv] * jnp.exp(g_last[hv]) + _dot_tn(k_dec, v_new_b[hv])


def _gdn_out_kernel(o_ref, z_ref, x_ref, nw_ref, wout_ref, out_ref, y_ref):
    for hv in range(GDN_NV):
        sl = slice(hv * GDN_DV, (hv + 1) * GDN_DV)
        o = o_ref[0, :, sl]
        y = _rms(o, nw_ref[...]) * _silu(z_ref[0, :, sl])
        y_ref[:, sl] = y.astype(BF16)
    out_ref[0] = x_ref[0] + _dot(y_ref[...], wout_ref[...])


def _params():
    return pltpu.CompilerParams(dimension_semantics=("arbitrary", "arbitrary"),
                                vmem_limit_bytes=VMEM_LIMIT)


def _tok_spec(width):
    return pl.BlockSpec((1, TM, width), lambda b, t: (b, t, 0))


def _const_spec(shape):
    nd = len(shape)
    return pl.BlockSpec(shape, lambda b, t: (0,) * nd, pipeline_mode=pl.Buffered(1))


def _shortconv_layer(x, nw, w_in, conv_w, w_out):
    bsz, s, d = x.shape
    return pl.pallas_call(
        _shortconv_kernel,
        grid=(bsz, s // TM),
        in_specs=[_tok_spec(d), _const_spec((1, d)), _const_spec(w_in.shape),
                  _const_spec(conv_w.shape), _const_spec(w_out.shape)],
        out_specs=_tok_spec(d),
        out_shape=jax.ShapeDtypeStruct(x.shape, F32),
        scratch_shapes=[pltpu.VMEM((HALO, d), F32)],
        compiler_params=_params(),
        name="shortconv_mixer",
    )(x, nw.reshape(1, d), w_in.astype(BF16), conv_w, w_out.astype(BF16))


def _ffn_layer(x, nw, w_up, conv_w, w_down, final_w, final):
    bsz, s, d = x.shape
    return pl.pallas_call(
        functools.partial(_ffn_kernel, final=final),
        grid=(bsz, s // TM),
        in_specs=[_tok_spec(d), _const_spec((1, d)), _const_spec(w_up.shape),
                  _const_spec(conv_w.shape), _const_spec(w_down.shape), _const_spec((1, d))],
        out_specs=_tok_spec(d),
        out_shape=jax.ShapeDtypeStruct(x.shape, F32),
        scratch_shapes=[pltpu.VMEM((HALO, 2 * D_FF), F32)],
        compiler_params=_params(),
        name="conv_ffn",
    )(x, nw.reshape(1, d), w_up.astype(BF16), conv_w, w_down.astype(BF16), final_w.reshape(1, d))


def _pad_lanes(v, offset=0):
    return jnp.zeros((1, LANES), F32).at[0, offset:offset + v.shape[0]].set(v)


def _gdn_layer(x, nw, w_in, conv_w, a_log, dt_bias, norm_w, w_out):
    bsz, s, d = x.shape
    w_qkv = w_in[:, :GDN_QKV_DIM].astype(BF16)
    w_z = w_in[:, GDN_QKV_DIM:GDN_QKV_DIM + GDN_VALUE_DIM].astype(BF16)
    w_ab = jnp.zeros((d, LANES), F32).at[:, :2 * GDN_NV].set(
        w_in[:, GDN_QKV_DIM + GDN_VALUE_DIM:]).astype(BF16)

    head_spec = lambda nh: pl.BlockSpec((1, nh, TM, LANES), lambda b, t: (b, 0, t, 0))
    q, k, v, z, gcol, grow = pl.pallas_call(
        _gdn_proj_kernel,
        grid=(bsz, s // TM),
        in_specs=[_tok_spec(d), _const_spec((1, d)), _const_spec(w_qkv.shape), _const_spec(w_z.shape),
                  _const_spec(w_ab.shape), _const_spec(conv_w.shape),
                  _const_spec((1, LANES)), _const_spec((1, LANES))],
        out_specs=[head_spec(GDN_NK), head_spec(GDN_NK), head_spec(GDN_NV), _tok_spec(GDN_VALUE_DIM),
                   _tok_spec(LANES), pl.BlockSpec((1, GDN_NV, TM), lambda b, t: (b, 0, t))],
        out_shape=[jax.ShapeDtypeStruct((bsz, GDN_NK, s, GDN_DK), F32),
                   jax.ShapeDtypeStruct((bsz, GDN_NK, s, GDN_DK), F32),
                   jax.ShapeDtypeStruct((bsz, GDN_NV, s, GDN_DV), F32),
                   jax.ShapeDtypeStruct((bsz, s, GDN_VALUE_DIM), F32),
                   jax.ShapeDtypeStruct((bsz, s, LANES), F32),
                   jax.ShapeDtypeStruct((bsz, GDN_NV, s), F32)],
        scratch_shapes=[pltpu.VMEM((HALO, GDN_QKV_DIM), F32)],
        compiler_params=_params(),
        name="gdn_proj",
    )(x, nw.reshape(1, d), w_qkv, w_z, w_ab, conv_w, _pad_lanes(a_log), _pad_lanes(dt_bias))

    chunk_heads = lambda nh: pl.BlockSpec((1, nh, CHUNK, LANES), lambda b, t: (b, 0, t, 0))
    o = pl.pallas_call(
        _gdn_rec_kernel,
        grid=(bsz, s // CHUNK),
        in_specs=[chunk_heads(GDN_NK), chunk_heads(GDN_NK), chunk_heads(GDN_NV),
                  pl.BlockSpec((1, CHUNK, LANES), lambda b, t: (b, t, 0)),
                  pl.BlockSpec((1, GDN_NV, CHUNK), lambda b, t: (b, 0, t))],
        out_specs=pl.BlockSpec((1, CHUNK, GDN_VALUE_DIM), lambda b, t: (b, t, 0)),
        out_shape=jax.ShapeDtypeStruct((bsz, s, GDN_VALUE_DIM), F32),
        scratch_shapes=[pltpu.VMEM((GDN_NV, GDN_DK, GDN_DV), F32)],
        compiler_params=_params(),
        name="gdn_recurrence",
    )(q, k, v, gcol, grow)

    return pl.pallas_call(
        _gdn_out_kernel,
        grid=(bsz, s // TM),
        in_specs=[_tok_spec(GDN_VALUE_DIM), _tok_spec(GDN_VALUE_DIM), _tok_spec(d),
                  _const_spec((1, GDN_DV)), _const_spec(w_out.shape)],
        out_specs=_tok_spec(d),
        out_shape=jax.ShapeDtypeStruct(x.shape, F32),
        scratch_shapes=[pltpu.VMEM((TM, GDN_VALUE_DIM), BF16)],
        compiler_params=_params(),
        name="gdn_out",
    )(o, z, x, norm_w.reshape(1, GDN_DV), w_out.astype(BF16))


def kernel(x, norm_mix, norm_ffn, sc_w_in, sc_conv, sc_w_out, gdn_w_in, gdn_conv, gdn_a_log,
           gdn_dt_bias, gdn_norm, gdn_w_out, ffn_w_up, ffn_conv, ffn_w_down, norm_final):
    for i in range(DEPTH):
        j = i // 2
        if i % 2 == 0:
            x = _shortconv_layer(x, norm_mix[i], sc_w_in[j], sc_conv[j], sc_w_out[j])
        else:
            x = _gdn_layer(x, norm_mix[i], gdn_w_in[j], gdn_conv[j], gdn_a_log[j], gdn_dt_bias[j],
                           gdn_norm[j], gdn_w_out[j])
        x = _ffn_layer(x, norm_ffn[i], ffn_w_up[i], ffn_conv[i], ffn_w_down[i], norm_final,
                       final=(i == DEPTH - 1))
    return x
```

```python
import functools

import jax
import jax.numpy as jnp
from jax import lax
from jax.experimental import pallas as pl
from jax.experimental.pallas import tpu as pltpu

F32 = jnp.float32
BF16 = jnp.bfloat16

D_MODEL = 1024
DEPTH = 4
EPS = 1e-6
GDN_NK = 8
GDN_NV = 16
GDN_DK = 128
GDN_DV = 128
GDN_KEY_DIM = GDN_NK * GDN_DK
GDN_VALUE_DIM = GDN_NV * GDN_DV
GDN_QKV_DIM = 2 * GDN_KEY_DIM + GDN_VALUE_DIM
D_FF = 2816

LANES = 128
SUBLANES = 8
HALO = 16
TM = 512
CHUNK = 128
REC_CHUNKS = 2
OUT_K_BLOCKS = 4
FF_CHUNK = D_FF
VMEM_LIMIT = 56 * 1024 * 1024


def _rms(x, w):
    ms = jnp.mean(x * x, axis=-1, keepdims=True)
    return x * lax.rsqrt(ms + EPS) * w


def _silu(x):
    return x * jax.nn.sigmoid(x)


def _dot(a, b):
    return jnp.dot(a, b, preferred_element_type=F32)


def _dot_nt(a, b):
    return lax.dot_general(a, b, (((1,), (1,)), ((), ())), preferred_element_type=F32)


def _shift_rows(p, carry, s):
    t, c = p.shape
    groups = t // SUBLANES
    rot = pltpu.roll(p.reshape(groups, SUBLANES, c), s, 1)
    carry_rot = pltpu.roll(carry[HALO - SUBLANES:], s, 0).reshape(1, SUBLANES, c)
    prev = jnp.concatenate([carry_rot, rot[:groups - 1]], axis=0)
    sub = lax.broadcasted_iota(jnp.int32, rot.shape, 1)
    return jnp.where(sub < s, prev, rot).reshape(t, c)


def _causal_conv(p, carry, w):
    taps = w.shape[0]
    acc = p * w[taps - 1:taps]
    for s in range(1, taps):
        acc = acc + _shift_rows(p, carry, s) * w[taps - 1 - s:taps - s]
    return acc


def _zero_carry_at_sequence_start(carry_ref):
    @pl.when(pl.program_id(1) == 0)
    def _():
        carry_ref[...] = jnp.zeros_like(carry_ref)


def _shortconv_kernel(x_ref, nw_ref, win_ref, cw_ref, wout_ref, o_ref, carry_ref):
    _zero_carry_at_sequence_start(carry_ref)
    x = x_ref[0]
    h = _rms(x, nw_ref[...]).astype(BF16)
    d = D_MODEL
    b_gate = _dot(h, win_ref[:, 0:d])
    c_gate = _dot(h, win_ref[:, d:2 * d])
    xv = _dot(h, win_ref[:, 2 * d:3 * d])
    p = c_gate * xv
    conv = _causal_conv(p, carry_ref[...], cw_ref[...])
    carry_ref[...] = p[TM - HALO:]
    y = (b_gate * conv).astype(BF16)
    o_ref[0] = x + _dot(y, wout_ref[...])


def _ffn_kernel(x_ref, nw_ref, wup_ref, cw_ref, wdn_ref, fw_ref, o_ref, carry_ref, *, final):
    _zero_carry_at_sequence_start(carry_ref)
    x = x_ref[0]
    h = _rms(x, nw_ref[...]).astype(BF16)
    acc = x
    for c in range(D_FF // FF_CHUNK):
        g_sl = slice(c * FF_CHUNK, (c + 1) * FF_CHUNK)
        v_sl = slice(D_FF + c * FF_CHUNK, D_FF + (c + 1) * FF_CHUNK)
        ug = _dot(h, wup_ref[:, g_sl])
        uv = _dot(h, wup_ref[:, v_sl])
        cg = _causal_conv(ug, carry_ref[:, g_sl], cw_ref[:, g_sl])
        cv = _causal_conv(uv, carry_ref[:, v_sl], cw_ref[:, v_sl])
        carry_ref[:, g_sl] = ug[TM - HALO:]
        carry_ref[:, v_sl] = uv[TM - HALO:]
        act = (_silu(cg) * cv).astype(BF16)
        acc = acc + _dot(act, wdn_ref[g_sl, :])
    if final:
        acc = _rms(acc, fw_ref[...])
    o_ref[0] = acc


def _split3_bf16(x):
    hi = x.astype(BF16)
    r1 = x - hi.astype(F32)
    mid = r1.astype(BF16)
    lo = (r1 - mid.astype(F32)).astype(BF16)
    return hi, mid, lo


def _gdn_proj_kernel(x_ref, nw_ref, win_ref, wab_ref, cw_ref, alog_ref, dtb_ref,
                     q_ref, k_ref, v_ref, z_ref, gcol_ref, grow_ref, carry_ref):
    _zero_carry_at_sequence_start(carry_ref)
    x = x_ref[0]
    h = _rms(x, nw_ref[...]).astype(BF16)
    group = GDN_KEY_DIM
    heads_per_group = group // LANES
    for c in range(GDN_QKV_DIM // group):
        sl = slice(c * group, (c + 1) * group)
        pre = _dot(h, win_ref[:, sl])
        act = _silu(_causal_conv(pre, carry_ref[:, sl], cw_ref[:, sl]))
        carry_ref[:, sl] = pre[TM - HALO:]
        for hh in range(heads_per_group):
            ah = act[:, hh * LANES:(hh + 1) * LANES]
            if c < 2:
                ah = ah * lax.rsqrt(jnp.sum(ah * ah, axis=-1, keepdims=True) + EPS)
            if c == 0:
                q_ref[0, hh] = (ah * (GDN_DK ** -0.5)).astype(q_ref.dtype)
            elif c == 1:
                k_ref[0, hh] = ah.astype(k_ref.dtype)
            else:
                v_ref[0, (c - 2) * heads_per_group + hh] = ah.astype(v_ref.dtype)
    z_ref[0] = _dot(h, win_ref[:, GDN_QKV_DIM:GDN_QKV_DIM + GDN_VALUE_DIM]).astype(z_ref.dtype)

    ab = _dot(h, wab_ref[...])
    lane = lax.broadcasted_iota(jnp.int32, ab.shape, 1)
    sp_in = ab + dtb_ref[...]
    softplus = jnp.maximum(sp_in, 0.0) + jnp.log1p(jnp.exp(-jnp.abs(sp_in)))
    g = -jnp.exp(alog_ref[...]) * softplus
    gate = jnp.where(lane < GDN_NV, g, jax.nn.sigmoid(ab))

    r = lax.broadcasted_iota(jnp.int32, (CHUNK, CHUNK), 0)
    cc = lax.broadcasted_iota(jnp.int32, (CHUNK, CHUNK), 1)
    tri = (r >= cc).astype(BF16)
    for blk in range(TM // CHUNK):
        rows = slice(blk * CHUNK, (blk + 1) * CHUNK)
        gate_blk = gate[rows]
        hi, mid, lo = _split3_bf16(gate_blk)
        gc = _dot(tri, hi) + _dot(tri, mid) + _dot(tri, lo)
        gcol_ref[0, rows, :] = jnp.where(cc < GDN_NV, gc, gate_blk)
        grow_ref[0, :, rows] = gc.T[:GDN_NV]


def _gdn_rec_kernel(q_ref, k_ref, v_ref, gcol_ref, grow_ref, o_ref, state_ref):
    @pl.when(pl.program_id(1) == 0)
    def _():
        state_ref[...] = jnp.zeros_like(state_ref)

    n = CHUNK
    r = lax.broadcasted_iota(jnp.int32, (n, n), 0)
    c = lax.broadcasted_iota(jnp.int32, (n, n), 1)
    tril = r >= c
    strict = r > c
    eye = (r == c).astype(F32)
    n_levels = n.bit_length() - 1

    def pair_mask(lvl):
        same_pair = (r >> (lvl + 1)) == (c >> (lvl + 1))
        return same_pair & (((r >> lvl) & 1) == 1) & (((c >> lvl) & 1) == 0)

    rep = GDN_NV // GDN_NK
    chunks = range(REC_CHUNKS)
    heads = range(GDN_NV)
    probs = [(ci, hv) for ci in chunks for hv in heads]
    rows = [slice(ci * n, (ci + 1) * n) for ci in chunks]

    qb = {(ci, hk): q_ref[0, hk, rows[ci], :] for ci in chunks for hk in range(GDN_NK)}
    kb = {(ci, hk): k_ref[0, hk, rows[ci], :] for ci in chunks for hk in range(GDN_NK)}
    kk = {p: _dot_nt(kb[p], kb[p]) for p in kb}
    qk = {p: _dot_nt(qb[p], kb[p]) for p in kb}
    k32 = {p: kb[p].astype(F32) for p in kb}
    gcol = [gcol_ref[0, rows[ci], :] for ci in chunks]
    grow = [grow_ref[0, :, rows[ci]] for ci in chunks]

    gc, beta, g_last, g_row, decay, low = {}, {}, {}, {}, {}, {}
    for (ci, hv) in probs:
        p = (ci, hv)
        gc[p] = gcol[ci][:, hv:hv + 1]
        beta[p] = gcol[ci][:, GDN_NV + hv:GDN_NV + hv + 1]
        g_last[p] = gcol[ci][n - 1:n, hv:hv + 1]
        g_row[p] = grow[ci][hv:hv + 1, :]
        diff = gc[p] - g_row[p]
        decay[p] = jnp.where(tril, jnp.exp(jnp.where(tril, diff, 0.0)), 0.0)
        low[p] = jnp.where(strict, beta[p] * kk[(ci, hv // rep)] * decay[p], 0.0)

    x_inv = {p: eye - jnp.where(pair_mask(0), low[p], 0.0) for p in probs}
    for lvl in range(1, n_levels):
        mask = pair_mask(lvl)
        xb = {p: x_inv[p].astype(BF16) for p in probs}
        cx = {p: _dot(jnp.where(mask, low[p], 0.0).astype(BF16), xb[p]).astype(BF16) for p in probs}
        x_inv = {p: x_inv[p] - _dot(xb[p], cx[p]) for p in probs}

    e_gc = {p: jnp.exp(gc[p]) for p in probs}
    sol = {}
    for (ci, hv) in probs:
        p = (ci, hv)
        v = v_ref[0, hv, rows[ci], :].astype(F32)
        rhs = jnp.concatenate([v * beta[p], k32[(ci, hv // rep)] * (beta[p] * e_gc[p])], axis=1)
        sol[p] = _dot(x_inv[p].astype(BF16), rhs.astype(BF16))
    q_dec = {(ci, hv): (qb[(ci, hv // rep)].astype(F32) * e_gc[(ci, hv)]).astype(BF16) for (ci, hv) in probs}
    qk_dec = {(ci, hv): (qk[(ci, hv // rep)] * decay[(ci, hv)]).astype(BF16) for (ci, hv) in probs}
    k_t = {p: k32[p].T for p in k32}
    k_dec_t = {(ci, hv): (k_t[(ci, hv // rep)] * jnp.exp(g_last[(ci, hv)] - g_row[(ci, hv)])).astype(BF16)
               for (ci, hv) in probs}

    state = [state_ref[hv] for hv in heads]
    for ci in chunks:
        state_b = [s.astype(BF16) for s in state]
        v_new_b = [(sol[(ci, hv)][:, :GDN_DV]
                    - _dot(sol[(ci, hv)][:, GDN_DV:].astype(BF16), state_b[hv])).astype(BF16) for hv in heads]
        for hv in heads:
            o = _dot(q_dec[(ci, hv)], state_b[hv]) + _dot(qk_dec[(ci, hv)], v_new_b[hv])
            o_ref[0, rows[ci], hv * GDN_DV:(hv + 1) * GDN_DV] = o.astype(o_ref.dtype)
        state = [state[hv] * jnp.exp(g_last[(ci, hv)]) + _dot(k_dec_t[(ci, hv)], v_new_b[hv]) for hv in heads]
    for hv in heads:
        state_ref[hv] = state[hv]


def _gdn_out_kernel(o_ref, z_ref, x_ref, nw_ref, wout_ref, out_ref):
    acc = x_ref[0]
    heads_per_block = GDN_NV // OUT_K_BLOCKS
    for blk in range(OUT_K_BLOCKS):
        ys = []
        for hv in range(blk * heads_per_block, (blk + 1) * heads_per_block):
            sl = slice(hv * GDN_DV, (hv + 1) * GDN_DV)
            o = o_ref[0, :, sl].astype(F32)
            ys.append((_rms(o, nw_ref[...]) * _silu(z_ref[0, :, sl].astype(F32))).astype(BF16))
        k_sl = slice(blk * heads_per_block * GDN_DV, (blk + 1) * heads_per_block * GDN_DV)
        acc = acc + _dot(jnp.concatenate(ys, axis=1), wout_ref[k_sl, :])
    out_ref[0] = acc


def _params():
    return pltpu.CompilerParams(dimension_semantics=("arbitrary", "arbitrary"),
                                vmem_limit_bytes=VMEM_LIMIT)


def _tok_spec(width):
    return pl.BlockSpec((1, TM, width), lambda b, t: (b, t, 0))


def _layer_spec(stacked, layer):
    _, rows, cols = stacked.shape
    return pl.BlockSpec((None, rows, cols), lambda b, t: (layer, 0, 0), pipeline_mode=pl.Buffered(1))


def _rows3(a):
    return a.reshape(a.shape[0], 1, a.shape[1])


def _shortconv_layer(x, j, nw, w_in, conv_w, w_out):
    bsz, s, d = x.shape
    return pl.pallas_call(
        _shortconv_kernel,
        grid=(bsz, s // TM),
        in_specs=[_tok_spec(d), _layer_spec(nw, 2 * j), _layer_spec(w_in, j), _layer_spec(conv_w, j),
                  _layer_spec(w_out, j)],
        out_specs=_tok_spec(d),
        out_shape=jax.ShapeDtypeStruct(x.shape, F32),
        scratch_shapes=[pltpu.VMEM((HALO, d), F32)],
        compiler_params=_params(),
        name="shortconv_mixer",
    )(x, nw, w_in, conv_w, w_out)


def _ffn_layer(x, i, nw, w_up, conv_w, w_down, final_w, final):
    bsz, s, d = x.shape
    return pl.pallas_call(
        functools.partial(_ffn_kernel, final=final),
        grid=(bsz, s // TM),
        in_specs=[_tok_spec(d), _layer_spec(nw, i), _layer_spec(w_up, i), _layer_spec(conv_w, i),
                  _layer_spec(w_down, i), _layer_spec(final_w, 0)],
        out_specs=_tok_spec(d),
        out_shape=jax.ShapeDtypeStruct(x.shape, F32),
        scratch_shapes=[pltpu.VMEM((HALO, 2 * D_FF), F32)],
        compiler_params=_params(),
        name="conv_ffn",
    )(x, nw, w_up, conv_w, w_down, final_w)


def _pad_lanes3(v):
    return jnp.zeros((v.shape[0], 1, LANES), F32).at[:, 0, :v.shape[1]].set(v)


def _gdn_layer(x, j, nw, w_in, w_ab, conv_w, a_log, dt_bias, norm_w, w_out):
    bsz, s, d = x.shape
    head_spec = lambda nh: pl.BlockSpec((1, nh, TM, LANES), lambda b, t: (b, 0, t, 0))
    q, k, v, z, gcol, grow = pl.pallas_call(
        _gdn_proj_kernel,
        grid=(bsz, s // TM),
        in_specs=[_tok_spec(d), _layer_spec(nw, 2 * j + 1), _layer_spec(w_in, j), _layer_spec(w_ab, j),
                  _layer_spec(conv_w, j), _layer_spec(a_log, j), _layer_spec(dt_bias, j)],
        out_specs=[head_spec(GDN_NK), head_spec(GDN_NK), head_spec(GDN_NV), _tok_spec(GDN_VALUE_DIM),
                   _tok_spec(LANES), pl.BlockSpec((1, GDN_NV, TM), lambda b, t: (b, 0, t))],
        out_shape=[jax.ShapeDtypeStruct((bsz, GDN_NK, s, GDN_DK), BF16),
                   jax.ShapeDtypeStruct((bsz, GDN_NK, s, GDN_DK), BF16),
                   jax.ShapeDtypeStruct((bsz, GDN_NV, s, GDN_DV), BF16),
                   jax.ShapeDtypeStruct((bsz, s, GDN_VALUE_DIM), BF16),
                   jax.ShapeDtypeStruct((bsz, s, LANES), F32),
                   jax.ShapeDtypeStruct((bsz, GDN_NV, s), F32)],
        scratch_shapes=[pltpu.VMEM((HALO, GDN_QKV_DIM), F32)],
        compiler_params=_params(),
        name="gdn_proj",
    )(x, nw, w_in, w_ab, conv_w, a_log, dt_bias)

    rec_rows = REC_CHUNKS * CHUNK
    chunk_heads = lambda nh: pl.BlockSpec((1, nh, rec_rows, LANES), lambda b, t: (b, 0, t, 0))
    o = pl.pallas_call(
        _gdn_rec_kernel,
        grid=(bsz, s // rec_rows),
        in_specs=[chunk_heads(GDN_NK), chunk_heads(GDN_NK), chunk_heads(GDN_NV),
                  pl.BlockSpec((1, rec_rows, LANES), lambda b, t: (b, t, 0)),
                  pl.BlockSpec((1, GDN_NV, rec_rows), lambda b, t: (b, 0, t))],
        out_specs=pl.BlockSpec((1, rec_rows, GDN_VALUE_DIM), lambda b, t: (b, t, 0)),
        out_shape=jax.ShapeDtypeStruct((bsz, s, GDN_VALUE_DIM), BF16),
        scratch_shapes=[pltpu.VMEM((GDN_NV, GDN_DK, GDN_DV), F32)],
        compiler_params=_params(),
        name="gdn_recurrence",
    )(q, k, v, gcol, grow)

    return pl.pallas_call(
        _gdn_out_kernel,
        grid=(bsz, s // TM),
        in_specs=[_tok_spec(GDN_VALUE_DIM), _tok_spec(GDN_VALUE_DIM), _tok_spec(d),
                  _layer_spec(norm_w, j), _layer_spec(w_out, j)],
        out_specs=_tok_spec(d),
        out_shape=jax.ShapeDtypeStruct(x.shape, F32),
        compiler_params=_params(),
        name="gdn_out",
    )(o, z, x, norm_w, w_out)


def kernel(x, norm_mix, norm_ffn, sc_w_in, sc_conv, sc_w_out, gdn_w_in, gdn_conv, gdn_a_log,
           gdn_dt_bias, gdn_norm, gdn_w_out, ffn_w_up, ffn_conv, ffn_w_down, norm_final):
    norm_mix, norm_ffn, gdn_norm = _rows3(norm_mix), _rows3(norm_ffn), _rows3(gdn_norm)
    norm_final = norm_final.reshape(1, 1, D_MODEL)
    a_log, dt_bias = _pad_lanes3(gdn_a_log), _pad_lanes3(gdn_dt_bias)
    gdn_w_ab = jnp.zeros((gdn_w_in.shape[0], D_MODEL, LANES), F32).at[:, :, :2 * GDN_NV].set(
        gdn_w_in[:, :, GDN_QKV_DIM + GDN_VALUE_DIM:]).astype(BF16)
    sc_w_in, sc_w_out, gdn_w_in, gdn_w_out, ffn_w_up, ffn_w_down = (
        w.astype(BF16) for w in (sc_w_in, sc_w_out, gdn_w_in, gdn_w_out, ffn_w_up, ffn_w_down))
    for i in range(DEPTH):
        j = i // 2
        if i % 2 == 0:
            x = _shortconv_layer(x, j, norm_mix, sc_w_in, sc_conv, sc_w_out)
        else:
            x = _gdn_layer(x, j, norm_mix, gdn_w_in, gdn_w_ab, gdn_conv, a_log, dt_bias, gdn_norm, gdn_w_out)
        x = _ffn_layer(x, i, norm_ffn, ffn_w_up, ffn_conv, ffn_w_down, norm_final, final=(i == DEPTH - 1))
    return x
```

```python
import functools

import jax
import jax.numpy as jnp
from jax import lax
from jax.experimental import pallas as pl
from jax.experimental.pallas import tpu as pltpu

F32 = jnp.float32
BF16 = jnp.bfloat16

D_MODEL = 1024
DEPTH = 4
EPS = 1e-6
GDN_NK = 8
GDN_NV = 16
GDN_DK = 128
GDN_DV = 128
GDN_KEY_DIM = GDN_NK * GDN_DK
GDN_VALUE_DIM = GDN_NV * GDN_DV
GDN_QKV_DIM = 2 * GDN_KEY_DIM + GDN_VALUE_DIM
D_FF = 2816

LANES = 128
SUBLANES = 8
BF16_ROWS = 16
HALO = 16
TM = 512
CHUNK = 128
REC_CHUNKS = 4
OUT_K_BLOCKS = 4
FF_CHUNK = D_FF
VMEM_LIMIT = 56 * 1024 * 1024


def _rms(x, w):
    ms = jnp.mean(x * x, axis=-1, keepdims=True)
    return x * lax.rsqrt(ms + EPS) * w


def _silu(x):
    h = 0.5 * x
    return h + h * jnp.tanh(h)


def _dot(a, b):
    return jnp.dot(a, b, preferred_element_type=F32)


def _dot_nt(a, b):
    return lax.dot_general(a, b, (((1,), (1,)), ((), ())), preferred_element_type=F32)


def _shift_rows(p, carry, s):
    t, c = p.shape
    groups = t // SUBLANES
    rot = pltpu.roll(p.reshape(groups, SUBLANES, c), s, 1)
    carry_rot = pltpu.roll(carry[HALO - SUBLANES:], s, 0).reshape(1, SUBLANES, c)
    prev = jnp.concatenate([carry_rot, rot[:groups - 1]], axis=0)
    sub = lax.broadcasted_iota(jnp.int32, rot.shape, 1)
    return jnp.where(sub < s, prev, rot).reshape(t, c)


def _causal_conv(p, carry, w):
    taps = w.shape[0]
    acc = p * w[taps - 1:taps]
    for s in range(1, taps):
        acc = acc + _shift_rows(p, carry, s) * w[taps - 1 - s:taps - s]
    return acc


def _zero_carry_at_sequence_start(carry_ref):
    @pl.when(pl.program_id(1) == 0)
    def _():
        carry_ref[...] = jnp.zeros_like(carry_ref)


def _shortconv_kernel(x_ref, nw_ref, win_ref, cw_ref, wout_ref, o_ref, carry_ref):
    _zero_carry_at_sequence_start(carry_ref)
    x = x_ref[0]
    h = _rms(x, nw_ref[...]).astype(BF16)
    d = D_MODEL
    b_gate = _dot(h, win_ref[:, 0:d])
    c_gate = _dot(h, win_ref[:, d:2 * d])
    xv = _dot(h, win_ref[:, 2 * d:3 * d])
    p = c_gate * xv
    conv = _causal_conv(p, carry_ref[...], cw_ref[...])
    carry_ref[...] = p[TM - HALO:]
    y = (b_gate * conv).astype(BF16)
    o_ref[0] = x + _dot(y, wout_ref[...])


def _ffn_kernel(x_ref, nw_ref, wup_ref, cw_ref, wdn_ref, fw_ref, o_ref, carry_ref, *, final):
    _zero_carry_at_sequence_start(carry_ref)
    x = x_ref[0]
    h = _rms(x, nw_ref[...]).astype(BF16)
    acc = x
    for c in range(D_FF // FF_CHUNK):
        g_sl = slice(c * FF_CHUNK, (c + 1) * FF_CHUNK)
        v_sl = slice(D_FF + c * FF_CHUNK, D_FF + (c + 1) * FF_CHUNK)
        ug = _dot(h, wup_ref[:, g_sl])
        uv = _dot(h, wup_ref[:, v_sl])
        cg = _causal_conv(ug, carry_ref[:, g_sl], cw_ref[:, g_sl])
        cv = _causal_conv(uv, carry_ref[:, v_sl], cw_ref[:, v_sl])
        carry_ref[:, g_sl] = ug[TM - HALO:]
        carry_ref[:, v_sl] = uv[TM - HALO:]
        act = (_silu(cg) * cv).astype(BF16)
        acc = acc + _dot(act, wdn_ref[g_sl, :])
    if final:
        acc = _rms(acc, fw_ref[...])
    o_ref[0] = acc


def _split3_bf16(x):
    hi = x.astype(BF16)
    r1 = x - hi.astype(F32)
    mid = r1.astype(BF16)
    lo = (r1 - mid.astype(F32)).astype(BF16)
    return hi, mid, lo


def _gdn_proj_kernel(x_ref, nw_ref, win_ref, wab_ref, cw_ref, alog_ref, dtb_ref,
                     q_ref, k_ref, v_ref, z_ref, gcol_ref, grow_ref, carry_ref):
    _zero_carry_at_sequence_start(carry_ref)
    x = x_ref[0]
    h = _rms(x, nw_ref[...]).astype(BF16)
    group = GDN_KEY_DIM
    heads_per_group = group // LANES
    for c in range(GDN_QKV_DIM // group):
        sl = slice(c * group, (c + 1) * group)
        pre = _dot(h, win_ref[:, sl])
        act = _silu(_causal_conv(pre, carry_ref[:, sl], cw_ref[:, sl]))
        carry_ref[:, sl] = pre[TM - HALO:]
        for hh in range(heads_per_group):
            ah = act[:, hh * LANES:(hh + 1) * LANES]
            if c < 2:
                ah = ah * lax.rsqrt(jnp.sum(ah * ah, axis=-1, keepdims=True) + EPS)
            if c == 0:
                q_ref[0, hh] = (ah * (GDN_DK ** -0.5)).astype(q_ref.dtype)
            elif c == 1:
                k_ref[0, hh] = ah.astype(k_ref.dtype)
            else:
                v_ref[0, (c - 2) * heads_per_group + hh] = ah.astype(v_ref.dtype)
    z_ref[0] = _dot(h, win_ref[:, GDN_QKV_DIM:GDN_QKV_DIM + GDN_VALUE_DIM]).astype(z_ref.dtype)

    ab = _dot(h, wab_ref[...])
    lane = lax.broadcasted_iota(jnp.int32, ab.shape, 1)
    sp_in = ab + dtb_ref[...]
    softplus = jnp.maximum(sp_in, 0.0) + jnp.log1p(jnp.exp(-jnp.abs(sp_in)))
    g = -jnp.exp(alog_ref[...]) * softplus
    gate = jnp.where(lane < GDN_NV, g, jax.nn.sigmoid(ab))

    r = lax.broadcasted_iota(jnp.int32, (CHUNK, CHUNK), 0)
    cc = lax.broadcasted_iota(jnp.int32, (CHUNK, CHUNK), 1)
    tri = (r >= cc).astype(BF16)
    for blk in range(TM // CHUNK):
        rows = slice(blk * CHUNK, (blk + 1) * CHUNK)
        gate_blk = gate[rows]
        hi, mid, lo = _split3_bf16(gate_blk)
        gc = _dot(tri, hi) + _dot(tri, mid) + _dot(tri, lo)
        gcol_ref[0, rows, :] = jnp.where(cc < GDN_NV, gc, gate_blk)
        grow_ref[0, :, rows] = gc.T[:GDN_NV]


def _gdn_rec_kernel(q_ref, k_ref, v_ref, gcol_ref, grow_ref, o_ref, state_ref):
    @pl.when(pl.program_id(1) == 0)
    def _():
        state_ref[...] = jnp.zeros_like(state_ref)

    n = CHUNK
    r = lax.broadcasted_iota(jnp.int32, (n, n), 0)
    c = lax.broadcasted_iota(jnp.int32, (n, n), 1)
    tril = r >= c
    strict = r > c
    eye = (r == c).astype(F32)
    n_levels = n.bit_length() - 1

    def pair_mask(lvl):
        same_pair = (r >> (lvl + 1)) == (c >> (lvl + 1))
        return same_pair & (((r >> lvl) & 1) == 1) & (((c >> lvl) & 1) == 0)

    rep = GDN_NV // GDN_NK
    chunks = range(REC_CHUNKS)
    heads = range(GDN_NV)
    probs = [(ci, hv) for ci in chunks for hv in heads]
    rows = [slice(ci * n, (ci + 1) * n) for ci in chunks]

    qb = {(ci, hk): q_ref[0, hk, rows[ci], :] for ci in chunks for hk in range(GDN_NK)}
    kb = {(ci, hk): k_ref[0, hk, rows[ci], :] for ci in chunks for hk in range(GDN_NK)}
    kk = {p: _dot_nt(kb[p], kb[p]) for p in kb}
    qk = {p: _dot_nt(qb[p], kb[p]) for p in kb}
    k32 = {p: kb[p].astype(F32) for p in kb}
    gcol = [gcol_ref[0, rows[ci], :] for ci in chunks]
    grow = [grow_ref[0, :, rows[ci]] for ci in chunks]

    gc, beta, g_last, g_row, decay, low = {}, {}, {}, {}, {}, {}
    for (ci, hv) in probs:
        p = (ci, hv)
        gc[p] = gcol[ci][:, hv:hv + 1]
        beta[p] = gcol[ci][:, GDN_NV + hv:GDN_NV + hv + 1]
        g_last[p] = gcol[ci][n - 1:n, hv:hv + 1]
        g_row[p] = grow[ci][hv:hv + 1, :]
        diff = gc[p] - g_row[p]
        decay[p] = jnp.where(tril, jnp.exp(jnp.where(tril, diff, 0.0)), 0.0)
        low[p] = jnp.where(strict, beta[p] * kk[(ci, hv // rep)] * decay[p], 0.0)

    x_inv = {p: eye - jnp.where(pair_mask(0), low[p], 0.0) for p in probs}
    for lvl in range(1, n_levels):
        mask = pair_mask(lvl)
        xb = {p: x_inv[p].astype(BF16) for p in probs}
        cx = {p: _dot(jnp.where(mask, low[p], 0.0).astype(BF16), xb[p]).astype(BF16) for p in probs}
        x_inv = {p: x_inv[p] - _dot(xb[p], cx[p]) for p in probs}

    e_gc = {p: jnp.exp(gc[p]) for p in probs}
    sol = {}
    for (ci, hv) in probs:
        p = (ci, hv)
        v = v_ref[0, hv, rows[ci], :].astype(F32)
        rhs = jnp.concatenate([v * beta[p], k32[(ci, hv // rep)] * (beta[p] * e_gc[p])], axis=1)
        sol[p] = _dot(x_inv[p].astype(BF16), rhs.astype(BF16))
    q_dec = {(ci, hv): (qb[(ci, hv // rep)].astype(F32) * e_gc[(ci, hv)]).astype(BF16) for (ci, hv) in probs}
    qk_dec = {(ci, hv): (qk[(ci, hv // rep)] * decay[(ci, hv)]).astype(BF16) for (ci, hv) in probs}
    k_t = {p: k32[p].T for p in k32}
    k_dec_t = {(ci, hv): (k_t[(ci, hv // rep)] * jnp.exp(g_last[(ci, hv)] - g_row[(ci, hv)])).astype(BF16)
               for (ci, hv) in probs}

    state = [state_ref[hv] for hv in heads]
    for ci in chunks:
        state_b = [s.astype(BF16) for s in state]
        v_new_b = [(sol[(ci, hv)][:, :GDN_DV]
                    - _dot(sol[(ci, hv)][:, GDN_DV:].astype(BF16), state_b[hv])).astype(BF16) for hv in heads]
        for hv in heads:
            o = _dot(q_dec[(ci, hv)], state_b[hv]) + _dot(qk_dec[(ci, hv)], v_new_b[hv])
            o_ref[0, rows[ci], hv * GDN_DV:(hv + 1) * GDN_DV] = o.astype(o_ref.dtype)
        state = [state[hv] * jnp.exp(g_last[(ci, hv)]) + _dot(k_dec_t[(ci, hv)], v_new_b[hv]) for hv in heads]
    for hv in heads:
        state_ref[hv] = state[hv]


def _gdn_out_kernel(o_ref, z_ref, x_ref, nw_ref, wout_ref, out_ref):
    acc = x_ref[0]
    heads_per_block = GDN_NV // OUT_K_BLOCKS
    for blk in range(OUT_K_BLOCKS):
        ys = []
        for hv in range(blk * heads_per_block, (blk + 1) * heads_per_block):
            sl = slice(hv * GDN_DV, (hv + 1) * GDN_DV)
            o = o_ref[0, :, sl].astype(F32)
            ys.append((_rms(o, nw_ref[...]) * _silu(z_ref[0, :, sl].astype(F32))).astype(BF16))
        k_sl = slice(blk * heads_per_block * GDN_DV, (blk + 1) * heads_per_block * GDN_DV)
        acc = acc + _dot(jnp.concatenate(ys, axis=1), wout_ref[k_sl, :])
    out_ref[0] = acc


def _params():
    return pltpu.CompilerParams(dimension_semantics=("arbitrary", "arbitrary"),
                                vmem_limit_bytes=VMEM_LIMIT)


def _tok_spec(width):
    return pl.BlockSpec((1, TM, width), lambda b, t: (b, t, 0))


def _layer_spec(stacked, layer):
    _, rows, cols = stacked.shape
    return pl.BlockSpec((None, rows, cols), lambda b, t: (layer, 0, 0), pipeline_mode=pl.Buffered(1))


def _whole_spec(a):
    return pl.BlockSpec(a.shape, lambda b, t: (0, 0), pipeline_mode=pl.Buffered(1))


def _rows3(a):
    return a.reshape(a.shape[0], 1, a.shape[1])


class _CastJob:
    def __init__(self, stacked, layer, grid):
        _, rows, cols = stacked.shape
        n_t = grid[1]
        steps = grid[0] * n_t
        n_chunks = max(c for c in range(1, steps + 1) if steps % c == 0 and rows % (c * BF16_ROWS) == 0)
        per_chunk = steps // n_chunks
        chunk = rows // n_chunks
        self.operand = stacked
        self.in_spec = pl.BlockSpec((None, chunk, cols), lambda b, t: (layer, (b * n_t + t) // per_chunk, 0))
        self.out_spec = pl.BlockSpec((chunk, cols), lambda b, t: ((b * n_t + t) // per_chunk, 0))
        self.out_shape = jax.ShapeDtypeStruct((rows, cols), BF16)


def _call_with_casts(body, *, n_in, n_out, jobs, **kwargs):
    n_jobs = len(jobs)

    def kern(*refs):
        ins, rest = refs[:n_in], refs[n_in:]
        cast_src, rest = rest[:n_jobs], rest[n_jobs:]
        outs, rest = rest[:n_out], rest[n_out:]
        cast_dst, scratch = rest[:n_jobs], rest[n_jobs:]
        for src_ref, dst_ref in zip(cast_src, cast_dst):
            dst_ref[...] = src_ref[...].astype(BF16)
        body(*ins, *outs, *scratch)

    in_specs = list(kwargs.pop("in_specs")) + [jb.in_spec for jb in jobs]
    out_specs = list(kwargs.pop("out_specs")) + [jb.out_spec for jb in jobs]
    out_shape = list(kwargs.pop("out_shape")) + [jb.out_shape for jb in jobs]
    call = pl.pallas_call(kern, in_specs=in_specs, out_specs=out_specs, out_shape=out_shape,
                          compiler_params=_params(), **kwargs)

    def run(*operands):
        res = call(*operands, *[jb.operand for jb in jobs])
        return res[:n_out], res[n_out:]
    return run


def _token_grid(x):
    return (x.shape[0], x.shape[1] // TM)


def _shortconv_layer(x, j, nw, w_in, conv_w, w_out, casts):
    d = x.shape[2]
    grid = _token_grid(x)
    (y,), cast = _call_with_casts(
        _shortconv_kernel, n_in=5, n_out=1, jobs=[_CastJob(a, l, grid) for a, l in casts],
        grid=grid,
        in_specs=[_tok_spec(d), _layer_spec(nw, 2 * j), _whole_spec(w_in), _layer_spec(conv_w, j),
                  _whole_spec(w_out)],
        out_specs=[_tok_spec(d)],
        out_shape=[jax.ShapeDtypeStruct(x.shape, F32)],
        scratch_shapes=[pltpu.VMEM((HALO, d), F32)],
        name="shortconv_mixer",
    )(x, nw, w_in, conv_w, w_out)
    return y, cast


def _ffn_layer(x, i, nw, w_up, conv_w, w_down, final_w, final, casts):
    d = x.shape[2]
    grid = _token_grid(x)
    (y,), cast = _call_with_casts(
        functools.partial(_ffn_kernel, final=final), n_in=6, n_out=1,
        jobs=[_CastJob(a, l, grid) for a, l in casts],
        grid=grid,
        in_specs=[_tok_spec(d), _layer_spec(nw, i), _whole_spec(w_up), _layer_spec(conv_w, i),
                  _whole_spec(w_down), _layer_spec(final_w, 0)],
        out_specs=[_tok_spec(d)],
        out_shape=[jax.ShapeDtypeStruct(x.shape, F32)],
        scratch_shapes=[pltpu.VMEM((HALO, 2 * D_FF), F32)],
        name="conv_ffn",
    )(x, nw, w_up, conv_w, w_down, final_w)
    return y, cast


def _pad_lanes3(v):
    return jnp.zeros((v.shape[0], 1, LANES), F32).at[:, 0, :v.shape[1]].set(v)


def _gdn_layer(x, j, nw, w_in, w_ab, conv_w, a_log, dt_bias, norm_w, w_out, casts):
    bsz, s, d = x.shape
    grid = _token_grid(x)
    head_spec = lambda nh: pl.BlockSpec((1, nh, TM, LANES), lambda b, t: (b, 0, t, 0))
    (q, k, v, z, gcol, grow), cast = _call_with_casts(
        _gdn_proj_kernel, n_in=7, n_out=6, jobs=[_CastJob(a, l, grid) for a, l in casts],
        grid=grid,
        in_specs=[_tok_spec(d), _layer_spec(nw, 2 * j + 1), _whole_spec(w_in), _layer_spec(w_ab, j),
                  _layer_spec(conv_w, j), _layer_spec(a_log, j), _layer_spec(dt_bias, j)],
        out_specs=[head_spec(GDN_NK), head_spec(GDN_NK), head_spec(GDN_NV), _tok_spec(GDN_VALUE_DIM),
                   _tok_spec(LANES), pl.BlockSpec((1, GDN_NV, TM), lambda b, t: (b, 0, t))],
        out_shape=[jax.ShapeDtypeStruct((bsz, GDN_NK, s, GDN_DK), BF16),
                   jax.ShapeDtypeStruct((bsz, GDN_NK, s, GDN_DK), BF16),
                   jax.ShapeDtypeStruct((bsz, GDN_NV, s, GDN_DV), BF16),
                   jax.ShapeDtypeStruct((bsz, s, GDN_VALUE_DIM), BF16),
                   jax.ShapeDtypeStruct((bsz, s, LANES), F32),
                   jax.ShapeDtypeStruct((bsz, GDN_NV, s), F32)],
        scratch_shapes=[pltpu.VMEM((HALO, GDN_QKV_DIM), F32)],
        name="gdn_proj",
    )(x, nw, w_in, w_ab, conv_w, a_log, dt_bias)

    rec_rows = REC_CHUNKS * CHUNK
    chunk_heads = lambda nh: pl.BlockSpec((1, nh, rec_rows, LANES), lambda b, t: (b, 0, t, 0))
    o = pl.pallas_call(
        _gdn_rec_kernel,
        grid=(bsz, s // rec_rows),
        in_specs=[chunk_heads(GDN_NK), chunk_heads(GDN_NK), chunk_heads(GDN_NV),
                  pl.BlockSpec((1, rec_rows, LANES), lambda b, t: (b, t, 0)),
                  pl.BlockSpec((1, GDN_NV, rec_rows), lambda b, t: (b, 0, t))],
        out_specs=pl.BlockSpec((1, rec_rows, GDN_VALUE_DIM), lambda b, t: (b, t, 0)),
        out_shape=jax.ShapeDtypeStruct((bsz, s, GDN_VALUE_DIM), BF16),
        scratch_shapes=[pltpu.VMEM((GDN_NV, GDN_DK, GDN_DV), F32)],
        compiler_params=_params(),
        name="gdn_recurrence",
    )(q, k, v, gcol, grow)

    y = pl.pallas_call(
        _gdn_out_kernel,
        grid=grid,
        in_specs=[_tok_spec(GDN_VALUE_DIM), _tok_spec(GDN_VALUE_DIM), _tok_spec(d),
                  _layer_spec(norm_w, j), _whole_spec(w_out)],
        out_specs=_tok_spec(d),
        out_shape=jax.ShapeDtypeStruct(x.shape, F32),
        compiler_params=_params(),
        name="gdn_out",
    )(o, z, x, norm_w, w_out)
    return y, cast


def kernel(x, norm_mix, norm_ffn, sc_w_in, sc_conv, sc_w_out, gdn_w_in, gdn_conv, gdn_a_log,
           gdn_dt_bias, gdn_norm, gdn_w_out, ffn_w_up, ffn_conv, ffn_w_down, norm_final):
    norm_mix, norm_ffn, gdn_norm = _rows3(norm_mix), _rows3(norm_ffn), _rows3(gdn_norm)
    norm_final = norm_final.reshape(1, 1, D_MODEL)
    a_log, dt_bias = _pad_lanes3(gdn_a_log), _pad_lanes3(gdn_dt_bias)
    gdn_w_ab = jnp.zeros((gdn_w_in.shape[0], D_MODEL, LANES), F32).at[:, :, :2 * GDN_NV].set(
        gdn_w_in[:, :, GDN_QKV_DIM + GDN_VALUE_DIM:]).astype(BF16)

    mix_w = (sc_w_in[0].astype(BF16), sc_w_out[0].astype(BF16))
    ffn_w = None
    for i in range(DEPTH):
        j = i // 2
        ffn_casts = [(ffn_w_up, i), (ffn_w_down, i)]
        if i % 2 == 0:
            x, ffn_w = _shortconv_layer(x, j, norm_mix, mix_w[0], sc_conv, mix_w[1], ffn_casts)
        else:
            x, ffn_w = _gdn_layer(x, j, norm_mix, mix_w[0], gdn_w_ab, gdn_conv, a_log, dt_bias, gdn_norm,
                                  mix_w[1], ffn_casts)
        if i + 1 == DEPTH:
            mix_casts = []
        elif i % 2 == 0:
            mix_casts = [(gdn_w_in, j), (gdn_w_out, j)]
        else:
            mix_casts = [(sc_w_in, j + 1), (sc_w_out, j + 1)]
        x, mix_w = _ffn_layer(x, i, norm_ffn, ffn_w[0], ffn_conv, ffn_w[1], norm_final,
                              final=(i + 1 == DEPTH), casts=mix_casts)
    return x
```

```python
import functools

import jax
import jax.numpy as jnp
from jax import lax
from jax.experimental import pallas as pl
from jax.experimental.pallas import tpu as pltpu

F32 = jnp.float32
BF16 = jnp.bfloat16

D_MODEL = 1024
DEPTH = 4
EPS = 1e-6
GDN_NK = 8
GDN_NV = 16
GDN_DK = 128
GDN_DV = 128
GDN_KEY_DIM = GDN_NK * GDN_DK
GDN_VALUE_DIM = GDN_NV * GDN_DV
GDN_QKV_DIM = 2 * GDN_KEY_DIM + GDN_VALUE_DIM
D_FF = 2816

LANES = 128
SUBLANES = 8
BF16_ROWS = 16
HALO = 16
TM = 512
CHUNK = 128
REC_CHUNKS = 4
OUT_K_BLOCKS = 4
FF_CHUNK = D_FF
VMEM_LIMIT = 56 * 1024 * 1024


def _rms(x, w):
    ms = jnp.mean(x * x, axis=-1, keepdims=True)
    return x * lax.rsqrt(ms + EPS) * w


def _silu(x):
    h = 0.5 * x
    return h + h * jnp.tanh(h)


def _dot(a, b):
    return jnp.dot(a, b, preferred_element_type=F32)


def _dot_nt(a, b):
    return lax.dot_general(a, b, (((1,), (1,)), ((), ())), preferred_element_type=F32)


def _shift_rows(p, carry, s):
    t, c = p.shape
    groups = t // SUBLANES
    rot = pltpu.roll(p.reshape(groups, SUBLANES, c), s, 1)
    carry_rot = pltpu.roll(carry[HALO - SUBLANES:], s, 0).reshape(1, SUBLANES, c)
    prev = jnp.concatenate([carry_rot, rot[:groups - 1]], axis=0)
    sub = lax.broadcasted_iota(jnp.int32, rot.shape, 1)
    return jnp.where(sub < s, prev, rot).reshape(t, c)


def _causal_conv(p, carry, w):
    taps = w.shape[0]
    acc = p * w[taps - 1:taps]
    for s in range(1, taps):
        acc = acc + _shift_rows(p, carry, s) * w[taps - 1 - s:taps - s]
    return acc


def _zero_carry_at_sequence_start(carry_ref):
    @pl.when(pl.program_id(1) == 0)
    def _():
        carry_ref[...] = jnp.zeros_like(carry_ref)


def _shortconv_kernel(x_ref, nw_ref, win_ref, cw_ref, wout_ref, o_ref, carry_ref):
    _zero_carry_at_sequence_start(carry_ref)
    x = x_ref[0]
    h = _rms(x, nw_ref[...]).astype(BF16)
    d = D_MODEL
    b_gate = _dot(h, win_ref[:, 0:d])
    c_gate = _dot(h, win_ref[:, d:2 * d])
    xv = _dot(h, win_ref[:, 2 * d:3 * d])
    p = c_gate * xv
    conv = _causal_conv(p, carry_ref[...], cw_ref[...])
    carry_ref[...] = p[TM - HALO:]
    y = (b_gate * conv).astype(BF16)
    o_ref[0] = x + _dot(y, wout_ref[...])


def _ffn_kernel(x_ref, nw_ref, wup_ref, cw_ref, wdn_ref, fw_ref, o_ref, carry_ref, *, final):
    _zero_carry_at_sequence_start(carry_ref)
    x = x_ref[0]
    h = _rms(x, nw_ref[...]).astype(BF16)
    acc = x
    for c in range(D_FF // FF_CHUNK):
        g_sl = slice(c * FF_CHUNK, (c + 1) * FF_CHUNK)
        v_sl = slice(D_FF + c * FF_CHUNK, D_FF + (c + 1) * FF_CHUNK)
        ug = _dot(h, wup_ref[:, g_sl])
        uv = _dot(h, wup_ref[:, v_sl])
        cg = _causal_conv(ug, carry_ref[:, g_sl], cw_ref[:, g_sl])
        cv = _causal_conv(uv, carry_ref[:, v_sl], cw_ref[:, v_sl])
        carry_ref[:, g_sl] = ug[TM - HALO:]
        carry_ref[:, v_sl] = uv[TM - HALO:]
        act = (_silu(cg) * cv).astype(BF16)
        acc = acc + _dot(act, wdn_ref[g_sl, :])
    if final:
        acc = _rms(acc, fw_ref[...])
    o_ref[0] = acc


def _split3_bf16(x):
    hi = x.astype(BF16)
    r1 = x - hi.astype(F32)
    mid = r1.astype(BF16)
    lo = (r1 - mid.astype(F32)).astype(BF16)
    return hi, mid, lo


def _gdn_proj_kernel(x_ref, nw_ref, win_ref, wab_ref, cw_ref, alog_ref, dtb_ref,
                     q_ref, k_ref, v_ref, z_ref, gcol_ref, grow_ref, carry_ref):
    _zero_carry_at_sequence_start(carry_ref)
    x = x_ref[0]
    h = _rms(x, nw_ref[...]).astype(BF16)
    group = GDN_KEY_DIM
    heads_per_group = group // LANES
    for c in range(GDN_QKV_DIM // group):
        sl = slice(c * group, (c + 1) * group)
        pre = _dot(h, win_ref[:, sl])
        act = _silu(_causal_conv(pre, carry_ref[:, sl], cw_ref[:, sl]))
        carry_ref[:, sl] = pre[TM - HALO:]
        for hh in range(heads_per_group):
            ah = act[:, hh * LANES:(hh + 1) * LANES]
            if c < 2:
                ah = ah * lax.rsqrt(jnp.sum(ah * ah, axis=-1, keepdims=True) + EPS)
            if c == 0:
                q_ref[0, hh] = (ah * (GDN_DK ** -0.5)).astype(q_ref.dtype)
            elif c == 1:
                k_ref[0, hh] = ah.astype(k_ref.dtype)
            else:
                v_ref[0, (c - 2) * heads_per_group + hh] = ah.astype(v_ref.dtype)
    z_ref[0] = _dot(h, win_ref[:, GDN_QKV_DIM:GDN_QKV_DIM + GDN_VALUE_DIM]).astype(z_ref.dtype)

    ab = _dot(h, wab_ref[...])
    lane = lax.broadcasted_iota(jnp.int32, ab.shape, 1)
    sp_in = ab + dtb_ref[...]
    softplus = jnp.maximum(sp_in, 0.0) + jnp.log1p(jnp.exp(-jnp.abs(sp_in)))
    g = -jnp.exp(alog_ref[...]) * softplus
    gate = jnp.where(lane < GDN_NV, g, jax.nn.sigmoid(ab))

    r = lax.broadcasted_iota(jnp.int32, (CHUNK, CHUNK), 0)
    cc = lax.broadcasted_iota(jnp.int32, (CHUNK, CHUNK), 1)
    tri = (r >= cc).astype(BF16)
    for blk in range(TM // CHUNK):
        rows = slice(blk * CHUNK, (blk + 1) * CHUNK)
        gate_blk = gate[rows]
        hi, mid, lo = _split3_bf16(gate_blk)
        gc = _dot(tri, hi) + _dot(tri, mid) + _dot(tri, lo)
        gcol_ref[0, rows, :] = jnp.where(cc < GDN_NV, gc, gate_blk)
        grow_ref[0, :, rows] = gc.T[:GDN_NV]


def _gdn_rec_kernel(q_ref, k_ref, v_ref, gcol_ref, grow_ref, o_ref, state_ref):
    @pl.when(pl.program_id(1) == 0)
    def _():
        state_ref[...] = jnp.zeros_like(state_ref)

    n = CHUNK
    r = lax.broadcasted_iota(jnp.int32, (n, n), 0)
    c = lax.broadcasted_iota(jnp.int32, (n, n), 1)
    tril = r >= c
    strict = r > c
    eye = (r == c).astype(F32)
    n_levels = n.bit_length() - 1

    def pair_mask(lvl):
        same_pair = (r >> (lvl + 1)) == (c >> (lvl + 1))
        return same_pair & (((r >> lvl) & 1) == 1) & (((c >> lvl) & 1) == 0)

    rep = GDN_NV // GDN_NK
    heads = range(GDN_NV)
    state = [state_ref[hv] for hv in heads]
    chunk_rows = [slice(ci * n, (ci + 1) * n) for ci in range(REC_CHUNKS)]
    qb_all = [[q_ref[0, hk, rows, :] for hk in range(GDN_NK)] for rows in chunk_rows]
    kb_all = [[k_ref[0, hk, rows, :] for hk in range(GDN_NK)] for rows in chunk_rows]
    kk_all = [[_dot_nt(kb[hk], kb[hk]) for hk in range(GDN_NK)] for kb in kb_all]
    qk_all = [[_dot_nt(qb[hk], kb[hk]) for hk in range(GDN_NK)] for qb, kb in zip(qb_all, kb_all)]
    for ci, rows in enumerate(chunk_rows):
        qb, kb, kk, qk = qb_all[ci], kb_all[ci], kk_all[ci], qk_all[ci]
        k32 = [kh.astype(F32) for kh in kb]
        gcol = gcol_ref[0, rows, :]
        grow = grow_ref[0, :, rows]

        gc = [gcol[:, hv:hv + 1] for hv in heads]
        beta = [gcol[:, GDN_NV + hv:GDN_NV + hv + 1] for hv in heads]
        g_last = [gcol[n - 1:n, hv:hv + 1] for hv in heads]
        g_row = [grow[hv:hv + 1, :] for hv in heads]
        decay, low = [], []
        for hv in heads:
            diff = gc[hv] - g_row[hv]
            decay.append(jnp.where(tril, jnp.exp(jnp.where(tril, diff, 0.0)), 0.0))
            low.append(jnp.where(strict, beta[hv] * kk[hv // rep] * decay[hv], 0.0))

        x_inv = [eye - jnp.where(pair_mask(0), low[hv], 0.0) for hv in heads]
        for lvl in range(1, n_levels):
            mask = pair_mask(lvl)
            xb = [x_inv[hv].astype(BF16) for hv in heads]
            cx = [_dot(jnp.where(mask, low[hv], 0.0).astype(BF16), xb[hv]).astype(BF16) for hv in heads]
            x_inv = [x_inv[hv] - _dot(xb[hv], cx[hv]) for hv in heads]

        e_gc = [jnp.exp(gc[hv]) for hv in heads]
        sol = []
        for hv in heads:
            v = v_ref[0, hv, rows, :].astype(F32)
            rhs = jnp.concatenate([v * beta[hv], k32[hv // rep] * (beta[hv] * e_gc[hv])], axis=1)
            sol.append(_dot(x_inv[hv].astype(BF16), rhs.astype(BF16)))
        q_dec = [(qb[hv // rep].astype(F32) * e_gc[hv]).astype(BF16) for hv in heads]
        qk_dec = [(qk[hv // rep] * decay[hv]).astype(BF16) for hv in heads]
        k_t = [kh.T for kh in k32]
        k_dec_t = [(k_t[hv // rep] * jnp.exp(g_last[hv] - g_row[hv])).astype(BF16) for hv in heads]

        state_b = [s.astype(BF16) for s in state]
        v_new_b = [(sol[hv][:, :GDN_DV] - _dot(sol[hv][:, GDN_DV:].astype(BF16), state_b[hv])).astype(BF16)
                   for hv in heads]
        for hv in heads:
            o = _dot(q_dec[hv], state_b[hv]) + _dot(qk_dec[hv], v_new_b[hv])
            o_ref[0, rows, hv * GDN_DV:(hv + 1) * GDN_DV] = o.astype(o_ref.dtype)
        state = [state[hv] * jnp.exp(g_last[hv]) + _dot(k_dec_t[hv], v_new_b[hv]) for hv in heads]
    for hv in heads:
        state_ref[hv] = state[hv]


def _gdn_out_kernel(o_ref, z_ref, x_ref, nw_ref, wout_ref, out_ref):
    acc = x_ref[0]
    heads_per_block = GDN_NV // OUT_K_BLOCKS
    for blk in range(OUT_K_BLOCKS):
        ys = []
        for hv in range(blk * heads_per_block, (blk + 1) * heads_per_block):
            sl = slice(hv * GDN_DV, (hv + 1) * GDN_DV)
            o = o_ref[0, :, sl].astype(F32)
            ys.append((_rms(o, nw_ref[...]) * _silu(z_ref[0, :, sl].astype(F32))).astype(BF16))
        k_sl = slice(blk * heads_per_block * GDN_DV, (blk + 1) * heads_per_block * GDN_DV)
        acc = acc + _dot(jnp.concatenate(ys, axis=1), wout_ref[k_sl, :])
    out_ref[0] = acc


def _params():
    return pltpu.CompilerParams(dimension_semantics=("arbitrary", "arbitrary"),
                                vmem_limit_bytes=VMEM_LIMIT)


def _tok_spec(width):
    return pl.BlockSpec((1, TM, width), lambda b, t: (b, t, 0))


def _layer_spec(stacked, layer):
    _, rows, cols = stacked.shape
    return pl.BlockSpec((None, rows, cols), lambda b, t: (layer, 0, 0), pipeline_mode=pl.Buffered(1))


def _whole_spec(a):
    return pl.BlockSpec(a.shape, lambda b, t: (0, 0), pipeline_mode=pl.Buffered(1))


def _rows3(a):
    return a.reshape(a.shape[0], 1, a.shape[1])


class _CastJob:
    def __init__(self, stacked, layer, grid):
        _, rows, cols = stacked.shape
        n_t = grid[1]
        steps = grid[0] * n_t
        n_chunks = max(c for c in range(1, steps + 1) if steps % c == 0 and rows % (c * BF16_ROWS) == 0)
        per_chunk = steps // n_chunks
        chunk = rows // n_chunks
        self.operand = stacked
        self.in_spec = pl.BlockSpec((None, chunk, cols), lambda b, t: (layer, (b * n_t + t) // per_chunk, 0))
        self.out_spec = pl.BlockSpec((chunk, cols), lambda b, t: ((b * n_t + t) // per_chunk, 0))
        self.out_shape = jax.ShapeDtypeStruct((rows, cols), BF16)


def _call_with_casts(body, *, n_in, n_out, jobs, **kwargs):
    n_jobs = len(jobs)

    def kern(*refs):
        ins, rest = refs[:n_in], refs[n_in:]
        cast_src, rest = rest[:n_jobs], rest[n_jobs:]
        outs, rest = rest[:n_out], rest[n_out:]
        cast_dst, scratch = rest[:n_jobs], rest[n_jobs:]
        for src_ref, dst_ref in zip(cast_src, cast_dst):
            dst_ref[...] = src_ref[...].astype(BF16)
        body(*ins, *outs, *scratch)

    in_specs = list(kwargs.pop("in_specs")) + [jb.in_spec for jb in jobs]
    out_specs = list(kwargs.pop("out_specs")) + [jb.out_spec for jb in jobs]
    out_shape = list(kwargs.pop("out_shape")) + [jb.out_shape for jb in jobs]
    call = pl.pallas_call(kern, in_specs=in_specs, out_specs=out_specs, out_shape=out_shape,
                          compiler_params=_params(), **kwargs)

    def run(*operands):
        res = call(*operands, *[jb.operand for jb in jobs])
        return res[:n_out], res[n_out:]
    return run


def _token_grid(x):
    return (x.shape[0], x.shape[1] // TM)


def _shortconv_layer(x, j, nw, w_in, conv_w, w_out, casts):
    d = x.shape[2]
    grid = _token_grid(x)
    (y,), cast = _call_with_casts(
        _shortconv_kernel, n_in=5, n_out=1, jobs=[_CastJob(a, l, grid) for a, l in casts],
        grid=grid,
        in_specs=[_tok_spec(d), _layer_spec(nw, 2 * j), _whole_spec(w_in), _layer_spec(conv_w, j),
                  _whole_spec(w_out)],
        out_specs=[_tok_spec(d)],
        out_shape=[jax.ShapeDtypeStruct(x.shape, F32)],
        scratch_shapes=[pltpu.VMEM((HALO, d), F32)],
        name="shortconv_mixer",
    )(x, nw, w_in, conv_w, w_out)
    return y, cast


def _ffn_layer(x, i, nw, w_up, conv_w, w_down, final_w, final, casts):
    d = x.shape[2]
    grid = _token_grid(x)
    (y,), cast = _call_with_casts(
        functools.partial(_ffn_kernel, final=final), n_in=6, n_out=1,
        jobs=[_CastJob(a, l, grid) for a, l in casts],
        grid=grid,
        in_specs=[_tok_spec(d), _layer_spec(nw, i), _whole_spec(w_up), _layer_spec(conv_w, i),
                  _whole_spec(w_down), _layer_spec(final_w, 0)],
        out_specs=[_tok_spec(d)],
        out_shape=[jax.ShapeDtypeStruct(x.shape, F32)],
        scratch_shapes=[pltpu.VMEM((HALO, 2 * D_FF), F32)],
        name="conv_ffn",
    )(x, nw, w_up, conv_w, w_down, final_w)
    return y, cast


def _pad_lanes3(v):
    return jnp.zeros((v.shape[0], 1, LANES), F32).at[:, 0, :v.shape[1]].set(v)


def _gdn_layer(x, j, nw, w_in, w_ab, conv_w, a_log, dt_bias, norm_w, w_out, casts):
    bsz, s, d = x.shape
    grid = _token_grid(x)
    head_spec = lambda nh: pl.BlockSpec((1, nh, TM, LANES), lambda b, t: (b, 0, t, 0))
    (q, k, v, z, gcol, grow), cast = _call_with_casts(
        _gdn_proj_kernel, n_in=7, n_out=6, jobs=[_CastJob(a, l, grid) for a, l in casts],
        grid=grid,
        in_specs=[_tok_spec(d), _layer_spec(nw, 2 * j + 1), _whole_spec(w_in), _layer_spec(w_ab, j),
                  _layer_spec(conv_w, j), _layer_spec(a_log, j), _layer_spec(dt_bias, j)],
        out_specs=[head_spec(GDN_NK), head_spec(GDN_NK), head_spec(GDN_NV), _tok_spec(GDN_VALUE_DIM),
                   _tok_spec(LANES), pl.BlockSpec((1, GDN_NV, TM), lambda b, t: (b, 0, t))],
        out_shape=[jax.ShapeDtypeStruct((bsz, GDN_NK, s, GDN_DK), BF16),
                   jax.ShapeDtypeStruct((bsz, GDN_NK, s, GDN_DK), BF16),
                   jax.ShapeDtypeStruct((bsz, GDN_NV, s, GDN_DV), BF16),
                   jax.ShapeDtypeStruct((bsz, s, GDN_VALUE_DIM), BF16),
                   jax.ShapeDtypeStruct((bsz, s, LANES), F32),
                   jax.ShapeDtypeStruct((bsz, GDN_NV, s), F32)],
        scratch_shapes=[pltpu.VMEM((HALO, GDN_QKV_DIM), F32)],
        name="gdn_proj",
    )(x, nw, w_in, w_ab, conv_w, a_log, dt_bias)

    rec_rows = REC_CHUNKS * CHUNK
    chunk_heads = lambda nh: pl.BlockSpec((1, nh, rec_rows, LANES), lambda b, t: (b, 0, t, 0))
    o = pl.pallas_call(
        _gdn_rec_kernel,
        grid=(bsz, s // rec_rows),
        in_specs=[chunk_heads(GDN_NK), chunk_heads(GDN_NK), chunk_heads(GDN_NV),
                  pl.BlockSpec((1, rec_rows, LANES), lambda b, t: (b, t, 0)),
                  pl.BlockSpec((1, GDN_NV, rec_rows), lambda b, t: (b, 0, t))],
        out_specs=pl.BlockSpec((1, rec_rows, GDN_VALUE_DIM), lambda b, t: (b, t, 0)),
        out_shape=jax.ShapeDtypeStruct((bsz, s, GDN_VALUE_DIM), BF16),
        scratch_shapes=[pltpu.VMEM((GDN_NV, GDN_DK, GDN_DV), F32)],
        compiler_params=_params(),
        name="gdn_recurrence",
    )(q, k, v, gcol, grow)

    y = pl.pallas_call(
        _gdn_out_kernel,
        grid=grid,
        in_specs=[_tok_spec(GDN_VALUE_DIM), _tok_spec(GDN_VALUE_DIM), _tok_spec(d),
                  _layer_spec(norm_w, j), _whole_spec(w_out)],
        out_specs=_tok_spec(d),
        out_shape=jax.ShapeDtypeStruct(x.shape, F32),
        compiler_params=_params(),
        name="gdn_out",
    )(o, z, x, norm_w, w_out)
    return y, cast


def kernel(x, norm_mix, norm_ffn, sc_w_in, sc_conv, sc_w_out, gdn_w_in, gdn_conv, gdn_a_log,
           gdn_dt_bias, gdn_norm, gdn_w_out, ffn_w_up, ffn_conv, ffn_w_down, norm_final):
    norm_mix, norm_ffn, gdn_norm = _rows3(norm_mix), _rows3(norm_ffn), _rows3(gdn_norm)
    norm_final = norm_final.reshape(1, 1, D_MODEL)
    a_log, dt_bias = _pad_lanes3(gdn_a_log), _pad_lanes3(gdn_dt_bias)
    gdn_w_ab = jnp.zeros((gdn_w_in.shape[0], D_MODEL, LANES), F32).at[:, :, :2 * GDN_NV].set(
        gdn_w_in[:, :, GDN_QKV_DIM + GDN_VALUE_DIM:]).astype(BF16)

    mix_w = (sc_w_in[0].astype(BF16), sc_w_out[0].astype(BF16))
    ffn_w = None
    for i in range(DEPTH):
        j = i // 2
        ffn_casts = [(ffn_w_up, i), (ffn_w_down, i)]
        if i % 2 == 0:
            x, ffn_w = _shortconv_layer(x, j, norm_mix, mix_w[0], sc_conv, mix_w[1], ffn_casts)
        else:
            x, ffn_w = _gdn_layer(x, j, norm_mix, mix_w[0], gdn_w_ab, gdn_conv, a_log, dt_bias, gdn_norm,
                                  mix_w[1], ffn_casts)
        if i + 1 == DEPTH:
            mix_casts = []
        elif i % 2 == 0:
            mix_casts = [(gdn_w_in, j), (gdn_w_out, j)]
        else:
            mix_casts = [(sc_w_in, j + 1), (sc_w_out, j + 1)]
        x, mix_w = _ffn_layer(x, i, norm_ffn, ffn_w[0], ffn_conv, ffn_w[1], norm_final,
                              final=(i + 1 == DEPTH), casts=mix_casts)
    return x
```

```python
import functools

import jax
import jax.numpy as jnp
from jax import lax
from jax.experimental import pallas as pl
from jax.experimental.pallas import tpu as pltpu

F32 = jnp.float32
BF16 = jnp.bfloat16

D_MODEL = 1024
DEPTH = 4
EPS = 1e-6
GDN_NK = 8
GDN_NV = 16
GDN_DK = 128
GDN_DV = 128
GDN_KEY_DIM = GDN_NK * GDN_DK
GDN_VALUE_DIM = GDN_NV * GDN_DV
GDN_QKV_DIM = 2 * GDN_KEY_DIM + GDN_VALUE_DIM
D_FF = 2816

LANES = 128
SUBLANES = 8
BF16_ROWS = 16
HALO = 16
TM = 512
CHUNK = 128
REC_CHUNKS = 4
OUT_K_BLOCKS = 4
FF_CHUNK = D_FF
VMEM_LIMIT = 56 * 1024 * 1024


def _rms(x, w):
    ms = jnp.mean(x * x, axis=-1, keepdims=True)
    return x * lax.rsqrt(ms + EPS) * w


def _silu(x):
    h = 0.5 * x
    return h + h * jnp.tanh(h)


def _dot(a, b):
    return jnp.dot(a, b, preferred_element_type=F32)


def _dot_nt(a, b):
    return lax.dot_general(a, b, (((1,), (1,)), ((), ())), preferred_element_type=F32)


def _shift_rows(p, carry, s):
    t, c = p.shape
    groups = t // SUBLANES
    rot = pltpu.roll(p.reshape(groups, SUBLANES, c), s, 1)
    carry_rot = pltpu.roll(carry[HALO - SUBLANES:], s, 0).reshape(1, SUBLANES, c)
    prev = jnp.concatenate([carry_rot, rot[:groups - 1]], axis=0)
    sub = lax.broadcasted_iota(jnp.int32, rot.shape, 1)
    return jnp.where(sub < s, prev, rot).reshape(t, c)


def _causal_conv(p, carry, w):
    taps = w.shape[0]
    acc = p * w[taps - 1:taps]
    for s in range(1, taps):
        acc = acc + _shift_rows(p, carry, s) * w[taps - 1 - s:taps - s]
    return acc


def _causal_conv4(p, carry_p, carry_b, w):
    s1 = _shift_rows(p, carry_p, 1)
    b = p * w[1:2] + s1 * w[0:1]
    return p * w[3:4] + s1 * w[2:3] + _shift_rows(b, carry_b, 2), b


def _zero_carry_at_sequence_start(*carry_refs):
    @pl.when(pl.program_id(1) == 0)
    def _():
        for carry_ref in carry_refs:
            carry_ref[...] = jnp.zeros_like(carry_ref)


def _shortconv_kernel(x_ref, nw_ref, win_ref, cw_ref, wout_ref, o_ref, carry_ref):
    _zero_carry_at_sequence_start(carry_ref)
    x = x_ref[0]
    h = _rms(x, nw_ref[...]).astype(BF16)
    d = D_MODEL
    b_gate = _dot(h, win_ref[:, 0:d])
    c_gate = _dot(h, win_ref[:, d:2 * d])
    xv = _dot(h, win_ref[:, 2 * d:3 * d])
    p = c_gate * xv
    conv = _causal_conv(p, carry_ref[...], cw_ref[...])
    carry_ref[...] = p[TM - HALO:]
    y = (b_gate * conv).astype(BF16)
    o_ref[0] = x + _dot(y, wout_ref[...])


def _ffn_kernel(x_ref, nw_ref, wup_ref, cw_ref, wdn_ref, fw_ref, o_ref, carry_ref, *, final):
    _zero_carry_at_sequence_start(carry_ref)
    x = x_ref[0]
    h = _rms(x, nw_ref[...]).astype(BF16)
    acc = x
    for c in range(D_FF // FF_CHUNK):
        g_sl = slice(c * FF_CHUNK, (c + 1) * FF_CHUNK)
        v_sl = slice(D_FF + c * FF_CHUNK, D_FF + (c + 1) * FF_CHUNK)
        ug = _dot(h, wup_ref[:, g_sl])
        uv = _dot(h, wup_ref[:, v_sl])
        cg = _causal_conv(ug, carry_ref[:, g_sl], cw_ref[:, g_sl])
        cv = _causal_conv(uv, carry_ref[:, v_sl], cw_ref[:, v_sl])
        carry_ref[:, g_sl] = ug[TM - HALO:]
        carry_ref[:, v_sl] = uv[TM - HALO:]
        act = (_silu(cg) * cv).astype(BF16)
        acc = acc + _dot(act, wdn_ref[g_sl, :])
    if final:
        acc = _rms(acc, fw_ref[...])
    o_ref[0] = acc


def _split3_bf16(x):
    hi = x.astype(BF16)
    r1 = x - hi.astype(F32)
    mid = r1.astype(BF16)
    lo = (r1 - mid.astype(F32)).astype(BF16)
    return hi, mid, lo


def _gdn_proj_kernel(x_ref, nw_ref, win_ref, wab_ref, cw_ref, alog_ref, dtb_ref,
                     q_ref, k_ref, v_ref, z_ref, gcol_ref, grow_ref, carry_ref, carry_b_ref):
    _zero_carry_at_sequence_start(carry_ref, carry_b_ref)
    x = x_ref[0]
    h = _rms(x, nw_ref[...]).astype(BF16)
    group = GDN_KEY_DIM
    heads_per_group = group // LANES
    for c in range(GDN_QKV_DIM // group):
        sl = slice(c * group, (c + 1) * group)
        pre = _dot(h, win_ref[:, sl])
        half, part = _causal_conv4(pre, carry_ref[:, sl], carry_b_ref[:, sl], 0.5 * cw_ref[:, sl])
        carry_ref[:, sl] = pre[TM - HALO:]
        carry_b_ref[:, sl] = part[TM - HALO:]
        act = half + half * jnp.tanh(half)
        for hh in range(heads_per_group):
            ah = act[:, hh * LANES:(hh + 1) * LANES]
            if c < 2:
                inv_norm = lax.rsqrt(jnp.sum(ah * ah, axis=-1, keepdims=True) + EPS)
                ah = ah * (inv_norm * (GDN_DK ** -0.5) if c == 0 else inv_norm)
            if c == 0:
                q_ref[0, hh] = ah.astype(q_ref.dtype)
            elif c == 1:
                k_ref[0, hh] = ah.astype(k_ref.dtype)
            else:
                v_ref[0, (c - 2) * heads_per_group + hh] = ah.astype(v_ref.dtype)
    z_ref[0] = _dot(h, win_ref[:, GDN_QKV_DIM:GDN_QKV_DIM + GDN_VALUE_DIM]).astype(z_ref.dtype)

    ab = _dot(h, wab_ref[...])
    lane = lax.broadcasted_iota(jnp.int32, ab.shape, 1)
    sp_in = ab + dtb_ref[...]
    softplus = jnp.maximum(sp_in, 0.0) + jnp.log1p(jnp.exp(-jnp.abs(sp_in)))
    g = -jnp.exp(alog_ref[...]) * softplus
    gate = jnp.where(lane < GDN_NV, g, jax.nn.sigmoid(ab))

    r = lax.broadcasted_iota(jnp.int32, (CHUNK, CHUNK), 0)
    cc = lax.broadcasted_iota(jnp.int32, (CHUNK, CHUNK), 1)
    tri = (r >= cc).astype(BF16)
    for blk in range(TM // CHUNK):
        rows = slice(blk * CHUNK, (blk + 1) * CHUNK)
        gate_blk = gate[rows]
        hi, mid, lo = _split3_bf16(gate_blk)
        gc = _dot(tri, hi) + _dot(tri, mid) + _dot(tri, lo)
        gcol_ref[0, rows, :] = jnp.where(cc < GDN_NV, gc, gate_blk)
        grow_ref[0, :, rows] = gc.T[:GDN_NV]


def _gdn_rec_kernel(q_ref, k_ref, v_ref, gcol_ref, grow_ref, o_ref, state_ref):
    @pl.when(pl.program_id(1) == 0)
    def _():
        state_ref[...] = jnp.zeros_like(state_ref)

    n = CHUNK
    r = lax.broadcasted_iota(jnp.int32, (n, n), 0)
    c = lax.broadcasted_iota(jnp.int32, (n, n), 1)
    tril = r >= c
    strict = r > c
    eye = (r == c).astype(F32)
    n_levels = n.bit_length() - 1

    def pair_mask(lvl):
        same_pair = (r >> (lvl + 1)) == (c >> (lvl + 1))
        return same_pair & (((r >> lvl) & 1) == 1) & (((c >> lvl) & 1) == 0)

    rep = GDN_NV // GDN_NK
    heads = range(GDN_NV)
    state = [state_ref[hv] for hv in heads]
    chunk_rows = [slice(ci * n, (ci + 1) * n) for ci in range(REC_CHUNKS)]
    qb_all = [[q_ref[0, hk, rows, :] for hk in range(GDN_NK)] for rows in chunk_rows]
    kb_all = [[k_ref[0, hk, rows, :] for hk in range(GDN_NK)] for rows in chunk_rows]
    kk_all = [[_dot_nt(kb[hk], kb[hk]) for hk in range(GDN_NK)] for kb in kb_all]
    qk_all = [[_dot_nt(qb[hk], kb[hk]) for hk in range(GDN_NK)] for qb, kb in zip(qb_all, kb_all)]
    for ci, rows in enumerate(chunk_rows):
        qb, kb, kk, qk = qb_all[ci], kb_all[ci], kk_all[ci], qk_all[ci]
        k32 = [kh.astype(F32) for kh in kb]
        gcol = gcol_ref[0, rows, :]
        grow = grow_ref[0, :, rows]

        gc = [gcol[:, hv:hv + 1] for hv in heads]
        beta = [gcol[:, GDN_NV + hv:GDN_NV + hv + 1] for hv in heads]
        g_last = [gcol[n - 1:n, hv:hv + 1] for hv in heads]
        g_row = [grow[hv:hv + 1, :] for hv in heads]
        decay, low = [], []
        for hv in heads:
            diff = gc[hv] - g_row[hv]
            decay.append(jnp.where(tril, jnp.exp(jnp.where(tril, diff, 0.0)), 0.0))
            low.append(jnp.where(strict, beta[hv] * kk[hv // rep] * decay[hv], 0.0))

        x_inv = [eye - jnp.where(pair_mask(0), low[hv], 0.0) for hv in heads]
        for lvl in range(1, n_levels):
            mask = pair_mask(lvl)
            xb = [x_inv[hv].astype(BF16) for hv in heads]
            cx = [_dot(jnp.where(mask, low[hv], 0.0).astype(BF16), xb[hv]).astype(BF16) for hv in heads]
            x_inv = [x_inv[hv] - _dot(xb[hv], cx[hv]) for hv in heads]

        e_gc = [jnp.exp(gc[hv]) for hv in heads]
        sol = []
        for hv in heads:
            v = v_ref[0, hv, rows, :].astype(F32)
            rhs = jnp.concatenate([v * beta[hv], k32[hv // rep] * (beta[hv] * e_gc[hv])], axis=1)
            sol.append(_dot(x_inv[hv].astype(BF16), rhs.astype(BF16)))
        q_dec = [(qb[hv // rep].astype(F32) * e_gc[hv]).astype(BF16) for hv in heads]
        qk_dec = [(qk[hv // rep] * decay[hv]).astype(BF16) for hv in heads]
        k_t = [kh.T for kh in k32]
        k_dec_t = [(k_t[hv // rep] * jnp.exp(g_last[hv] - g_row[hv])).astype(BF16) for hv in heads]

        state_b = [s.astype(BF16) for s in state]
        v_new_b = [(sol[hv][:, :GDN_DV] - _dot(sol[hv][:, GDN_DV:].astype(BF16), state_b[hv])).astype(BF16)
                   for hv in heads]
        for hv in heads:
            o = _dot(q_dec[hv], state_b[hv]) + _dot(qk_dec[hv], v_new_b[hv])
            o_ref[0, rows, hv * GDN_DV:(hv + 1) * GDN_DV] = o.astype(o_ref.dtype)
        state = [state[hv] * jnp.exp(g_last[hv]) + _dot(k_dec_t[hv], v_new_b[hv]) for hv in heads]
    for hv in heads:
        state_ref[hv] = state[hv]


def _gdn_out_kernel(o_ref, z_ref, x_ref, nw_ref, wout_ref, out_ref):
    acc = x_ref[0]
    heads_per_block = GDN_NV // OUT_K_BLOCKS
    for blk in range(OUT_K_BLOCKS):
        ys = []
        for hv in range(blk * heads_per_block, (blk + 1) * heads_per_block):
            sl = slice(hv * GDN_DV, (hv + 1) * GDN_DV)
            o = o_ref[0, :, sl].astype(F32)
            ys.append((_rms(o, nw_ref[...]) * _silu(z_ref[0, :, sl].astype(F32))).astype(BF16))
        k_sl = slice(blk * heads_per_block * GDN_DV, (blk + 1) * heads_per_block * GDN_DV)
        acc = acc + _dot(jnp.concatenate(ys, axis=1), wout_ref[k_sl, :])
    out_ref[0] = acc


def _params():
    return pltpu.CompilerParams(dimension_semantics=("arbitrary", "arbitrary"),
                                vmem_limit_bytes=VMEM_LIMIT)


def _tok_spec(width):
    return pl.BlockSpec((1, TM, width), lambda b, t: (b, t, 0))


def _layer_spec(stacked, layer):
    _, rows, cols = stacked.shape
    return pl.BlockSpec((None, rows, cols), lambda b, t: (layer, 0, 0), pipeline_mode=pl.Buffered(1))


def _whole_spec(a):
    return pl.BlockSpec(a.shape, lambda b, t: (0, 0), pipeline_mode=pl.Buffered(1))


def _rows3(a):
    return a.reshape(a.shape[0], 1, a.shape[1])


class _CastJob:
    def __init__(self, stacked, layer, grid):
        _, rows, cols = stacked.shape
        n_t = grid[1]
        steps = grid[0] * n_t
        n_chunks = max(c for c in range(1, steps + 1) if steps % c == 0 and rows % (c * BF16_ROWS) == 0)
        per_chunk = steps // n_chunks
        chunk = rows // n_chunks
        self.operand = stacked
        self.in_spec = pl.BlockSpec((None, chunk, cols), lambda b, t: (layer, (b * n_t + t) // per_chunk, 0))
        self.out_spec = pl.BlockSpec((chunk, cols), lambda b, t: ((b * n_t + t) // per_chunk, 0))
        self.out_shape = jax.ShapeDtypeStruct((rows, cols), BF16)


def _call_with_casts(body, *, n_in, n_out, jobs, **kwargs):
    n_jobs = len(jobs)

    def kern(*refs):
        ins, rest = refs[:n_in], refs[n_in:]
        cast_src, rest = rest[:n_jobs], rest[n_jobs:]
        outs, rest = rest[:n_out], rest[n_out:]
        cast_dst, scratch = rest[:n_jobs], rest[n_jobs:]
        for src_ref, dst_ref in zip(cast_src, cast_dst):
            dst_ref[...] = src_ref[...].astype(BF16)
        body(*ins, *outs, *scratch)

    in_specs = list(kwargs.pop("in_specs")) + [jb.in_spec for jb in jobs]
    out_specs = list(kwargs.pop("out_specs")) + [jb.out_spec for jb in jobs]
    out_shape = list(kwargs.pop("out_shape")) + [jb.out_shape for jb in jobs]
    call = pl.pallas_call(kern, in_specs=in_specs, out_specs=out_specs, out_shape=out_shape,
                          compiler_params=_params(), **kwargs)

    def run(*operands):
        res = call(*operands, *[jb.operand for jb in jobs])
        return res[:n_out], res[n_out:]
    return run


def _token_grid(x):
    return (x.shape[0], x.shape[1] // TM)


def _shortconv_layer(x, j, nw, w_in, conv_w, w_out, casts):
    d = x.shape[2]
    grid = _token_grid(x)
    (y,), cast = _call_with_casts(
        _shortconv_kernel, n_in=5, n_out=1, jobs=[_CastJob(a, l, grid) for a, l in casts],
        grid=grid,
        in_specs=[_tok_spec(d), _layer_spec(nw, 2 * j), _whole_spec(w_in), _layer_spec(conv_w, j),
                  _whole_spec(w_out)],
        out_specs=[_tok_spec(d)],
        out_shape=[jax.ShapeDtypeStruct(x.shape, F32)],
        scratch_shapes=[pltpu.VMEM((HALO, d), F32)],
        name="shortconv_mixer",
    )(x, nw, w_in, conv_w, w_out)
    return y, cast


def _ffn_layer(x, i, nw, w_up, conv_w, w_down, final_w, final, casts):
    d = x.shape[2]
    grid = _token_grid(x)
    (y,), cast = _call_with_casts(
        functools.partial(_ffn_kernel, final=final), n_in=6, n_out=1,
        jobs=[_CastJob(a, l, grid) for a, l in casts],
        grid=grid,
        in_specs=[_tok_spec(d), _layer_spec(nw, i), _whole_spec(w_up), _layer_spec(conv_w, i),
                  _whole_spec(w_down), _layer_spec(final_w, 0)],
        out_specs=[_tok_spec(d)],
        out_shape=[jax.ShapeDtypeStruct(x.shape, F32)],
        scratch_shapes=[pltpu.VMEM((HALO, 2 * D_FF), F32)],
        name="conv_ffn",
    )(x, nw, w_up, conv_w, w_down, final_w)
    return y, cast


def _pad_lanes3(v):
    return jnp.zeros((v.shape[0], 1, LANES), F32).at[:, 0, :v.shape[1]].set(v)


def _gdn_layer(x, j, nw, w_in, w_ab, conv_w, a_log, dt_bias, norm_w, w_out, casts):
    bsz, s, d = x.shape
    grid = _token_grid(x)
    head_spec = lambda nh: pl.BlockSpec((1, nh, TM, LANES), lambda b, t: (b, 0, t, 0))
    (q, k, v, z, gcol, grow), cast = _call_with_casts(
        _gdn_proj_kernel, n_in=7, n_out=6, jobs=[_CastJob(a, l, grid) for a, l in casts],
        grid=grid,
        in_specs=[_tok_spec(d), _layer_spec(nw, 2 * j + 1), _whole_spec(w_in), _layer_spec(w_ab, j),
                  _layer_spec(conv_w, j), _layer_spec(a_log, j), _layer_spec(dt_bias, j)],
        out_specs=[head_spec(GDN_NK), head_spec(GDN_NK), head_spec(GDN_NV), _tok_spec(GDN_VALUE_DIM),
                   _tok_spec(LANES), pl.BlockSpec((1, GDN_NV, TM), lambda b, t: (b, 0, t))],
        out_shape=[jax.ShapeDtypeStruct((bsz, GDN_NK, s, GDN_DK), BF16),
                   jax.ShapeDtypeStruct((bsz, GDN_NK, s, GDN_DK), BF16),
                   jax.ShapeDtypeStruct((bsz, GDN_NV, s, GDN_DV), BF16),
                   jax.ShapeDtypeStruct((bsz, s, GDN_VALUE_DIM), BF16),
                   jax.ShapeDtypeStruct((bsz, s, LANES), F32),
                   jax.ShapeDtypeStruct((bsz, GDN_NV, s), F32)],
        scratch_shapes=[pltpu.VMEM((HALO, GDN_QKV_DIM), F32), pltpu.VMEM((HALO, GDN_QKV_DIM), F32)],
        name="gdn_proj",
    )(x, nw, w_in, w_ab, conv_w, a_log, dt_bias)

    rec_rows = REC_CHUNKS * CHUNK
    chunk_heads = lambda nh: pl.BlockSpec((1, nh, rec_rows, LANES), lambda b, t: (b, 0, t, 0))
    o = pl.pallas_call(
        _gdn_rec_kernel,
        grid=(bsz, s // rec_rows),
        in_specs=[chunk_heads(GDN_NK), chunk_heads(GDN_NK), chunk_heads(GDN_NV),
                  pl.BlockSpec((1, rec_rows, LANES), lambda b, t: (b, t, 0)),
                  pl.BlockSpec((1, GDN_NV, rec_rows), lambda b, t: (b, 0, t))],
        out_specs=pl.BlockSpec((1, rec_rows, GDN_VALUE_DIM), lambda b, t: (b, t, 0)),
        out_shape=jax.ShapeDtypeStruct((bsz, s, GDN_VALUE_DIM), BF16),
        scratch_shapes=[pltpu.VMEM((GDN_NV, GDN_DK, GDN_DV), F32)],
        compiler_params=_params(),
        name="gdn_recurrence",
    )(q, k, v, gcol, grow)

    y = pl.pallas_call(
        _gdn_out_kernel,
        grid=grid,
        in_specs=[_tok_spec(GDN_VALUE_DIM), _tok_spec(GDN_VALUE_DIM), _tok_spec(d),
                  _layer_spec(norm_w, j), _whole_spec(w_out)],
        out_specs=_tok_spec(d),
        out_shape=jax.ShapeDtypeStruct(x.shape, F32),
        compiler_params=_params(),
        name="gdn_out",
    )(o, z, x, norm_w, w_out)
    return y, cast


def kernel(x, norm_mix, norm_ffn, sc_w_in, sc_conv, sc_w_out, gdn_w_in, gdn_conv, gdn_a_log,
           gdn_dt_bias, gdn_norm, gdn_w_out, ffn_w_up, ffn_conv, ffn_w_down, norm_final):
    norm_mix, norm_ffn, gdn_norm = _rows3(norm_mix), _rows3(norm_ffn), _rows3(gdn_norm)
    norm_final = norm_final.reshape(1, 1, D_MODEL)
    a_log, dt_bias = _pad_lanes3(gdn_a_log), _pad_lanes3(gdn_dt_bias)
    gdn_w_ab = jnp.zeros((gdn_w_in.shape[0], D_MODEL, LANES), F32).at[:, :, :2 * GDN_NV].set(
        gdn_w_in[:, :, GDN_QKV_DIM + GDN_VALUE_DIM:]).astype(BF16)

    mix_w = (sc_w_in[0].astype(BF16), sc_w_out[0].astype(BF16))
    ffn_w = None
    for i in range(DEPTH):
        j = i // 2
        ffn_casts = [(ffn_w_up, i), (ffn_w_down, i)]
        if i % 2 == 0:
            x, ffn_w = _shortconv_layer(x, j, norm_mix, mix_w[0], sc_conv, mix_w[1], ffn_casts)
        else:
            x, ffn_w = _gdn_layer(x, j, norm_mix, mix_w[0], gdn_w_ab, gdn_conv, a_log, dt_bias, gdn_norm,
                                  mix_w[1], ffn_casts)
        if i + 1 == DEPTH:
            mix_casts = []
        elif i % 2 == 0:
            mix_casts = [(gdn_w_in, j), (gdn_w_out, j)]
        else:
            mix_casts = [(sc_w_in, j + 1), (sc_w_out, j + 1)]
        x, mix_w = _ffn_layer(x, i, norm_ffn, ffn_w[0], ffn_conv, ffn_w[1], norm_final,
                              final=(i + 1 == DEPTH), casts=mix_casts)
    return x
```
